```python
import math
import jax
import jax.numpy as jnp
from jax import lax
import numpy as np

D_MODEL = 2048
BATCH = 2
SEQ = 8192
DEPTH = 2

CTX_LEN = 256
GRID_W = 64
D_FF = 5632
N_MOD = 9
EPS = 1e-6
MIX_W = D_MODEL // 2
HG_DK = 128
HG_DV = 128
HG_HEADS = MIX_W // HG_DV
HG_KEY_W = HG_HEADS * HG_DK
HG_CHUNK = 64
CONV_W = MIX_W
CONV_K = 3
DA_HEAD_DIM = 64
DA_V_DIM = 2 * DA_HEAD_DIM
DA_HEADS = MIX_W // DA_V_DIM
DA_QK_W = DA_HEADS * 2 * DA_HEAD_DIM
Q_BLOCK = 128
ROPE_THETA = 10000.0
N_BRANCH = 3
IN_SPLIT_SIZES = (HG_KEY_W, HG_KEY_W, HG_KEY_W, MIX_W, MIX_W,
                  CONV_W, CONV_W, CONV_W,
                  DA_QK_W, DA_QK_W, MIX_W,
                  D_MODEL, D_MODEL, D_MODEL)
W_IN_COLS = sum(IN_SPLIT_SIZES)

kernel_name = 'hybrid_hgrn2_conv_diffattn_dit_block'


def rmsnorm(x, w):
    xf = x.astype(jnp.float32)
    y = xf * lax.rsqrt(jnp.mean(xf * xf, axis=-1, keepdims=True) + EPS)
    return (y * w.astype(jnp.float32)).astype(x.dtype)


def modulate(x, shift, scale):
    return x * (1 + scale) + shift


def half_ffn(h, nw, shift, scale, gate, w13, w2):
    xn = modulate(rmsnorm(h, nw), shift, scale)
    g, u = jnp.split(xn @ w13, 2, axis=-1)
    return h + 0.5 * gate * ((jax.nn.silu(g) * u) @ w2)


def split_cols(p):
    out, start = [], 0
    for size in IN_SPLIT_SIZES:
        out.append(p[..., start:start + size])
        start += size
    return out


def axial_rope_tables(n_tokens):
    rows = n_tokens // GRID_W
    row = jnp.repeat(jnp.arange(rows, dtype=jnp.int32), GRID_W)
    col = jnp.tile(jnp.arange(GRID_W, dtype=jnp.int32), rows)
    half = DA_HEAD_DIM // 2
    inv_freq = ROPE_THETA ** (-jnp.arange(0, half, 2, dtype=jnp.float32) / half)
    ang_r = row.astype(jnp.float32)[:, None] * inv_freq
    ang_c = col.astype(jnp.float32)[:, None] * inv_freq
    ang = jnp.concatenate([ang_r, ang_r, ang_c, ang_c], axis=-1)
    return jnp.cos(ang), jnp.sin(ang)


def rotate_half(x):
    x1, x2 = jnp.split(x, 2, axis=-1)
    return jnp.concatenate([-x2, x1], axis=-1)


def apply_axial_rope(x, cos, sin):
    half = DA_HEAD_DIM // 2
    rot = jnp.concatenate([rotate_half(x[..., :half]), rotate_half(x[..., half:])], axis=-1)
    cos = cos[None, :, None, None, :].astype(x.dtype)
    sin = sin[None, :, None, None, :].astype(x.dtype)
    return x * cos + rot * sin


def hgrn_lower_bounds(raw):
    cum = jnp.cumsum(jax.nn.softmax(raw.astype(jnp.float32), axis=1), axis=1)
    return cum - cum[:, :1]


def hgrn_log_forget(z, lb):
    lb = lb.reshape(HG_HEADS, HG_DK)
    return jnp.logaddexp(jnp.log(lb), jnp.log1p(-lb) + jax.nn.log_sigmoid(z.astype(jnp.float32)))


def hgrn_chunk_scan(q, k, v, logf, s0):
    b, t, h, _ = q.shape
    nc = t // HG_CHUNK

    def to_chunks(a):
        return a.reshape(b, nc, HG_CHUNK, h, a.shape[-1]).transpose(1, 0, 3, 2, 4)

    lower = jnp.tril(jnp.ones((HG_CHUNK, HG_CHUNK), dtype=bool))

    def step(state, inp):
        qc, kc, vc, gc = inp
        cum = jnp.cumsum(gc, axis=-2)
        rel = jnp.where(lower[:, :, None], cum[..., :, None, :] - cum[..., None, :, :], -jnp.inf)
        scores = jnp.einsum('bhtk,bhsk,bhtsk->bhts', qc, kc, jnp.exp(rel))
        o = (jnp.einsum('bhts,bhsv->bhtv', scores, vc)
             + jnp.einsum('bhtk,bhkv->bhtv', qc * jnp.exp(cum), state))
        last = cum[..., -1:, :]
        state = (jnp.exp(last[..., 0, :])[..., None] * state
                 + jnp.einsum('bhsk,bhsv->bhkv', kc * jnp.exp(last - cum), vc))
        return state, o

    s_final, o = lax.scan(step, s0, (to_chunks(q), to_chunks(k), to_chunks(v), to_chunks(logf)))
    o = o.transpose(1, 0, 3, 2, 4).reshape(b, t, h, v.shape[-1])
    return o, s_final


def hgrn_direction(q, z, v, lb, s0, reverse):
    logf = hgrn_log_forget(z, lb)
    k = -jnp.expm1(logf)
    q = q.astype(jnp.float32)
    v = v.astype(jnp.float32)
    if reverse:
        q, k, v, logf = (jnp.flip(a, axis=1) for a in (q, k, v, logf))
    o, s_final = hgrn_chunk_scan(q, k, v, logf, s0)
    if reverse:
        o = jnp.flip(o, axis=1)
    return o, s_final


def hgrn_readout(o, g, w):
    y = rmsnorm(o, w).reshape(o.shape[0], o.shape[1], MIX_W).astype(g.dtype)
    return y * jax.nn.silu(g)


def short_conv_branch(bg, cg, u, w):
    y = lax.conv_general_dilated(cg * u, w[:, None, :], window_strides=(1,), padding='SAME',
                                 dimension_numbers=('NWC', 'WIO', 'NWC'),
                                 feature_group_count=CONV_W)
    return bg * y


def diff_lambda(lam_p, lam_init):
    lp = lam_p.astype(jnp.float32)
    return jnp.exp(jnp.sum(lp[0] * lp[1])) - jnp.exp(jnp.sum(lp[2] * lp[3])) + lam_init


def diff_attention(q, k, v, lam):
    s = jnp.einsum('bqhpd,bshpd->bhpqs', q, k).astype(jnp.float32) * (DA_HEAD_DIM ** -0.5)
    p = jax.nn.softmax(s, axis=-1)
    w = (p[:, :, 0] - lam * p[:, :, 1]).astype(v.dtype)
    return jnp.einsum('bhqs,bshe->bqhe', w, v)


def da_readout(o, w, lam_init):
    return rmsnorm(o, w).reshape(o.shape[0], o.shape[1], MIX_W) * (1 - lam_init)


def merge_branches(branches, gates, w_branch, w_out):
    y = sum(jax.nn.sigmoid(g) * (br @ w_branch[i]) for i, (br, g) in enumerate(zip(branches, gates)))
    return y @ w_out


def token_mixer(xc, xl, w_in, lb_f, lb_b, hg_norm_w, conv_w, lam_p, lam_init, da_norm_w,
                w_branch, w_out, rope_cos, rope_sin, need_ctx):
    b, n, _ = xl.shape
    (hq_l, hf_l, hb_l, hi_l, hg_l, cb_l, cc_l, cu_l, aq_l, ak_l, av_l, ga_l, gb_l, gc_l) = split_cols(xl @ w_in)
    (hq_c, hf_c, hb_c, hi_c, hg_c, cb_c, cc_c, cu_c, aq_c, ak_c, av_c, ga_c, gb_c, gc_c) = split_cols(xc @ w_in)

    def hd(a):
        return a.reshape(a.shape[0], a.shape[1], HG_HEADS, -1)

    def qh(a):
        return a.reshape(a.shape[0], a.shape[1], DA_HEADS, 2, DA_HEAD_DIM)

    def vh(a):
        return a.reshape(a.shape[0], a.shape[1], DA_HEADS, DA_V_DIM)

    s0 = jnp.zeros((b, HG_HEADS, HG_DK, HG_DV), jnp.float32)
    o_cf, s_f = hgrn_direction(hd(hq_c), hd(hf_c), hd(hi_c), lb_f, s0, False)
    o_cb, s_b = hgrn_direction(hd(hq_c), hd(hb_c), hd(hi_c), lb_b, s0, True)
    o_lf, _ = hgrn_direction(hd(hq_l), hd(hf_l), hd(hi_l), lb_f, s_f, False)
    o_lb, _ = hgrn_direction(hd(hq_l), hd(hb_l), hd(hi_l), lb_b, s_b, True)
    a_lat = hgrn_readout(o_lf + o_lb, hg_l, hg_norm_w)

    b_lat = short_conv_branch(cb_l, cc_l, cu_l, conv_w)

    lam = diff_lambda(lam_p, lam_init)
    k_c, v_c = qh(ak_c), vh(av_c)
    q_l = apply_axial_rope(qh(aq_l), rope_cos, rope_sin)
    k_l = apply_axial_rope(qh(ak_l), rope_cos, rope_sin)
    k_all = jnp.concatenate([k_c, k_l], axis=1)
    v_all = jnp.concatenate([v_c, vh(av_l)], axis=1)
    nb = n // Q_BLOCK
    q_blocks = q_l.reshape(b, nb, Q_BLOCK, DA_HEADS, 2, DA_HEAD_DIM).swapaxes(0, 1)
    o_blocks = lax.map(lambda qb: diff_attention(qb, k_all, v_all, lam), q_blocks)
    o_l = o_blocks.swapaxes(0, 1).reshape(b, n, DA_HEADS, DA_V_DIM)
    c_lat = da_readout(o_l, da_norm_w, lam_init)

    y_lat = merge_branches((a_lat, b_lat, c_lat), (ga_l, gb_l, gc_l), w_branch, w_out)
    if not need_ctx:
        return y_lat, None
    a_ctx = hgrn_readout(o_cf + o_cb, hg_c, hg_norm_w)
    b_ctx = short_conv_branch(cb_c, cc_c, cu_c, conv_w)
    c_ctx_out = da_readout(diff_attention(qh(aq_c), k_c, v_c, lam), da_norm_w, lam_init)
    y_ctx = merge_branches((a_ctx, b_ctx, c_ctx_out), (ga_c, gb_c, gc_c), w_branch, w_out)
    return y_lat, y_ctx


def setup_inputs(seed: int = 0) -> dict:
    key = jax.random.key(seed)
    ks = jax.random.split(key, 18)

    def nrm(k, shape, std):
        return jax.random.normal(k, shape, jnp.float32) * std

    return {
        'x': nrm(ks[0], (BATCH, SEQ, D_MODEL), 1.0),
        'c': nrm(ks[1], (BATCH, D_MODEL), 1.0),
        'ctx': nrm(ks[2], (BATCH, CTX_LEN, D_MODEL), 1.0),
        'c_ctx': nrm(ks[3], (D_MODEL,), 1.0),
        'w_ada': nrm(ks[4], (DEPTH, D_MODEL, N_MOD * D_MODEL), 0.5 * D_MODEL ** -0.5),
        'b_ada': nrm(ks[5], (DEPTH, N_MOD * D_MODEL), 0.02),
        'norm_w': 1.0 + nrm(ks[6], (DEPTH, 3, D_MODEL), 0.02),
        'ffn_w13': nrm(ks[7], (DEPTH, 2, D_MODEL, 2 * D_FF), D_MODEL ** -0.5),
        'ffn_w2': nrm(ks[8], (DEPTH, 2, D_FF, D_MODEL), D_FF ** -0.5),
        'w_in': nrm(ks[9], (DEPTH, D_MODEL, W_IN_COLS), D_MODEL ** -0.5),
        'hgrn_lb': nrm(ks[10], (2, DEPTH, HG_KEY_W), 0.5),
        'hgrn_norm_w': 1.0 + nrm(ks[11], (DEPTH, HG_DV), 0.02),
        'conv_w': nrm(ks[12], (DEPTH, CONV_K, CONV_W), CONV_K ** -0.5),
        'da_lambda': nrm(ks[13], (DEPTH, 4, DA_HEAD_DIM), 0.1),
        'da_norm_w': 1.0 + nrm(ks[14], (DEPTH, DA_V_DIM), 0.02),
        'w_branch': nrm(ks[15], (DEPTH, N_BRANCH, MIX_W, D_MODEL), MIX_W ** -0.5),
        'w_out': nrm(ks[16], (DEPTH, D_MODEL, D_MODEL), D_MODEL ** -0.5),
        'final_norm_w': 1.0 + nrm(ks[17], (D_MODEL,), 0.02),
    }


def reference(x, c, ctx, c_ctx, w_ada, b_ada, norm_w, ffn_w13, ffn_w2, w_in, hgrn_lb,
              hgrn_norm_w, conv_w, da_lambda, da_norm_w, w_branch, w_out, final_norm_w):
    n_tokens = x.shape[1]
    rope_cos, rope_sin = axial_rope_tables(n_tokens)
    lower_bounds = hgrn_lower_bounds(hgrn_lb)
    silu_c = jax.nn.silu(c)
    silu_cc = jax.nn.silu(c_ctx)
    h_lat, h_ctx = x, ctx
    for layer in range(DEPTH):
        last = layer == DEPTH - 1
        lam_init = 0.8 - 0.6 * math.exp(-0.3 * layer)
        m_lat = jnp.split((silu_c @ w_ada[layer] + b_ada[layer])[:, None, :], N_MOD, axis=-1)
        m_ctx = jnp.split((silu_cc @ w_ada[layer] + b_ada[layer])[None, None, :], N_MOD, axis=-1)
        h_lat = half_ffn(h_lat, norm_w[layer, 0], m_lat[0], m_lat[1], m_lat[2],
                         ffn_w13[layer, 0], ffn_w2[layer, 0])
        h_ctx = half_ffn(h_ctx, norm_w[layer, 0], m_ctx[0], m_ctx[1], m_ctx[2],
                         ffn_w13[layer, 0], ffn_w2[layer, 0])
        xl = modulate(rmsnorm(h_lat, norm_w[layer, 1]), m_lat[3], m_lat[4])
        xc = modulate(rmsnorm(h_ctx, norm_w[layer, 1]), m_ctx[3], m_ctx[4])
        y_lat, y_ctx = token_mixer(xc, xl, w_in[layer], lower_bounds[0, layer], lower_bounds[1, layer],
                                   hgrn_norm_w[layer], conv_w[layer], da_lambda[layer], lam_init,
                                   da_norm_w[layer], w_branch[layer], w_out[layer],
                                   rope_cos, rope_sin, not last)
        h_lat = h_lat + m_lat[5] * y_lat
        h_lat = half_ffn(h_lat, norm_w[layer, 2], m_lat[6], m_lat[7], m_lat[8],
                         ffn_w13[layer, 1], ffn_w2[layer, 1])
        if not last:
            h_ctx = h_ctx + m_ctx[5] * y_ctx
            h_ctx = half_ffn(h_ctx, norm_w[layer, 2], m_ctx[6], m_ctx[7], m_ctx[8],
                             ffn_w13[layer, 1], ffn_w2[layer, 1])
    return rmsnorm(h_lat, final_norm_w)
```

```python
import functools
import math

import jax
import jax.numpy as jnp
import numpy as np
from jax import lax
from jax.experimental import pallas as pl
from jax.experimental.pallas import tpu as pltpu

_F32 = jnp.float32
_BF16 = jnp.bfloat16

EPS = 1e-6
N_MOD = 9
GRID_W = 64
ROPE_THETA = 10000.0
HEAD_W = 128
DA_HEAD_DIM = 64
HG_CHUNK = 64
HG_LEVELS = 6
V7X_VMEM_LIMIT = 56 * 1024 * 1024


def _dot(a, b):
    return jnp.dot(a, b, preferred_element_type=_F32)


def _dot_nt(a, b):
    return lax.dot_general(a, b, (((1,), (1,)), ((), ())), preferred_element_type=_F32)


def _dot_tn(a, b):
    return lax.dot_general(a, b, (((0,), (0,)), ((), ())), preferred_element_type=_F32)


def _params(semantics, vmem=V7X_VMEM_LIMIT):
    return pltpu.CompilerParams(dimension_semantics=semantics, vmem_limit_bytes=vmem)


def _norm_mod(h, nw, shift, scale):
    ms = jnp.mean(h * h, axis=-1, keepdims=True)
    return (h * lax.rsqrt(ms + EPS) * nw) * (1.0 + scale) + shift


def _ada_kernel(c_ref, w_ref, b_ref, o_ref):
    c = c_ref[...]
    s = (c * jax.nn.sigmoid(c)).astype(_BF16)
    o_ref[...] = _dot(s, w_ref[...].astype(_BF16)) + b_ref[...]


def _ada_mod(cvec, w_ada, b_ada, tn=1024):
    depth, d, nd = w_ada.shape
    rows = cvec.shape[0]
    return pl.pallas_call(
        _ada_kernel,
        grid=(depth, nd // tn),
        in_specs=[
            pl.BlockSpec((rows, d), lambda l, j: (0, 0)),
            pl.BlockSpec((None, d, tn), lambda l, j: (l, 0, j)),
            pl.BlockSpec((None, 1, tn), lambda l, j: (l, 0, j)),
        ],
        out_specs=pl.BlockSpec((None, rows, tn), lambda l, j: (l, 0, j)),
        out_shape=jax.ShapeDtypeStruct((depth, rows, nd), _F32),
        compiler_params=_params(("arbitrary", "arbitrary")),
        name="ada_mod",
    )(cvec, w_ada, b_ada.reshape(depth, 1, nd))


def _ffn_kernel(h_ref, mod_ref, nw_ref, w1_ref, w3_ref, w2_ref, *rest, mod_off, final_norm):
    if final_norm:
        fw_ref, o_ref, xn_ref = rest
    else:
        o_ref, xn_ref = rest
    j = pl.program_id(1)

    @pl.when(j == 0)
    def _():
        xn = _norm_mod(h_ref[...], nw_ref[...], mod_ref[mod_off:mod_off + 1, :],
                       mod_ref[mod_off + 1:mod_off + 2, :])
        xn_ref[...] = xn.astype(_BF16)
        o_ref[...] = jnp.zeros_like(o_ref)

    xn = xn_ref[...]
    g = _dot(xn, w1_ref[...])
    u = _dot(xn, w3_ref[...])
    a = (g * jax.nn.sigmoid(g) * u).astype(_BF16)
    o_ref[...] += _dot(a, w2_ref[...])

    @pl.when(j == pl.num_programs(1) - 1)
    def _():
        gate = mod_ref[mod_off + 2:mod_off + 3, :]
        out = h_ref[...] + 0.5 * gate * o_ref[...]
        if final_norm:
            ms = jnp.mean(out * out, axis=-1, keepdims=True)
            out = out * lax.rsqrt(ms + EPS) * fw_ref[...]
        o_ref[...] = out


def _ffn(h, n_rows, mod_l, nw, w13, w2, layer, which, mod_off, dims, final_w=None, tf=512):
    d = h.shape[1]
    f = w2.shape[2]
    tm = dims["tm"]
    nfb = f // tf
    in_specs = [
        pl.BlockSpec((tm, d), lambda i, j: (i, 0)),
        pl.BlockSpec((None, N_MOD, d), lambda i, j: (dims["mod_row"](i), 0, 0)),
        pl.BlockSpec((None, 1, d), lambda i, j: (nw, 0, 0)),
        pl.BlockSpec((None, None, d, tf), lambda i, j: (layer, which, 0, j)),
        pl.BlockSpec((None, None, d, tf), lambda i, j: (layer, which, 0, j + nfb)),
        pl.BlockSpec((None, None, tf, d), lambda i, j: (layer, which, j, 0)),
    ]
    args = [h, mod_l, dims["norm_w"], w13, w13, w2]
    if final_w is not None:
        in_specs.append(pl.BlockSpec((1, d), lambda i, j: (0, 0)))
        args.append(final_w.reshape(1, d))
    return pl.pallas_call(
        functools.partial(_ffn_kernel, mod_off=mod_off, final_norm=final_w is not None),
        grid=(n_rows // tm, nfb),
        in_specs=in_specs,
        out_specs=pl.BlockSpec((tm, d), lambda i, j: (i, 0)),
        out_shape=jax.ShapeDtypeStruct((n_rows, d), _F32),
        scratch_shapes=[pltpu.VMEM((tm, d), _BF16)],
        compiler_params=_params(("parallel", "arbitrary")),
        name="half_ffn",
    )(*args)


def _proj_kernel(h_ref, mod_ref, nw_ref, w_ref, o_ref, xn_ref):
    @pl.when(pl.program_id(1) == 0)
    def _():
        xn = _norm_mod(h_ref[...], nw_ref[...], mod_ref[3:4, :], mod_ref[4:5, :])
        xn_ref[...] = xn.astype(_BF16)

    o_ref[...] = _dot(xn_ref[...], w_ref[...]).astype(o_ref.dtype)


def _mixer_in(h, mod_l, nw, w_in, layer, dims, tn=1024):
    n, d = h.shape
    cols = w_in.shape[2]
    tm = dims["tm"]
    return pl.pallas_call(
        _proj_kernel,
        grid=(n // tm, cols // tn),
        in_specs=[
            pl.BlockSpec((tm, d), lambda i, j: (i, 0)),
            pl.BlockSpec((None, N_MOD, d), lambda i, j: (dims["mod_row"](i), 0, 0)),
            pl.BlockSpec((None, 1, d), lambda i, j: (nw, 0, 0)),
            pl.BlockSpec((None, d, tn), lambda i, j: (layer, 0, j)),
        ],
        out_specs=pl.BlockSpec((tm, tn), lambda i, j: (i, j)),
        out_shape=jax.ShapeDtypeStruct((n, cols), _BF16),
        scratch_shapes=[pltpu.VMEM((tm, d), _BF16)],
        compiler_params=_params(("parallel", "arbitrary")),
        name="mixer_in",
    )(h, mod_l, dims["norm_w"], w_in)


def _rope_kernel(x_ref, cos_ref, sin_ref, o_ref):
    cos = cos_ref[...]
    sin = sin_ref[...]
    lane = lax.broadcasted_iota(jnp.int32, cos.shape, 1)
    first = (lane % 32) < 16
    for g in range(x_ref.shape[1] // HEAD_W):
        sl = slice(g * HEAD_W, (g + 1) * HEAD_W)
        x = x_ref[:, sl].astype(_F32)
        rot = jnp.where(first, pltpu.roll(x, HEAD_W - 16, axis=1), pltpu.roll(x, 16, axis=1))
        o_ref[:, sl] = (x * cos + rot * sin).astype(o_ref.dtype)


def _rope_tables(n_tokens):
    rows = n_tokens // GRID_W
    row = jnp.repeat(jnp.arange(rows, dtype=jnp.int32), GRID_W)
    col = jnp.tile(jnp.arange(GRID_W, dtype=jnp.int32), rows)
    half = DA_HEAD_DIM // 2
    inv_freq = ROPE_THETA ** (-jnp.arange(0, half, 2, dtype=_F32) / half)
    ang_r = row.astype(_F32)[:, None] * inv_freq
    ang_c = col.astype(_F32)[:, None] * inv_freq
    ang = jnp.concatenate([ang_r, ang_r, ang_c, ang_c], axis=-1)
    ang = jnp.concatenate([ang, ang], axis=-1)
    sign = jnp.where((jnp.arange(HEAD_W) % 32) < 16, -1.0, 1.0)
    return jnp.cos(ang), jnp.sin(ang) * sign


def _rope(p, cos, sin, b, s, col0, tr=512):
    width = 2 * dims_mix_w(p)
    tr = min(tr, s)
    return pl.pallas_call(
        _rope_kernel,
        grid=(b * s // tr,),
        in_specs=[
            pl.BlockSpec((tr, width), lambda i: (i, col0 // width)),
            pl.BlockSpec((tr, HEAD_W), lambda i: (i % (s // tr), 0)),
            pl.BlockSpec((tr, HEAD_W), lambda i: (i % (s // tr), 0)),
        ],
        out_specs=pl.BlockSpec((tr, width), lambda i: (i, 0)),
        out_shape=jax.ShapeDtypeStruct((b * s, width), _BF16),
        compiler_params=_params(("parallel",)),
        name="axial_rope",
    )(p, cos, sin)


def dims_mix_w(p):
    return p.shape[1] // 17


def _hgrn_tables():
    L = HG_CHUNK
    w = np.zeros((2, (HG_LEVELS + 2) * L + 8, L), np.float32)
    lmap = np.full((2, L, L), HG_LEVELS + 1, np.int32)
    r = np.arange(L)
    for t in range(L):
        w[0, t, r <= t] = 1
        w[0, L + t, r > t] = 1
        w[1, t, r >= t] = 1
        w[1, L + t, r < t] = 1
        lmap[:, t, t] = 0
        for lvl in range(1, HG_LEVELS + 1):
            bs = (2 * L) >> lvl
            mid = (t // bs) * bs + bs // 2
            base = (lvl + 1) * L + t
            if t >= mid:
                w[0, base, (r >= mid) & (r <= t)] = 1
                w[1, base, (r >= mid) & (r < t)] = 1
            else:
                w[0, base, (r > t) & (r < mid)] = 1
                w[1, base, (r >= t) & (r < mid)] = 1
            for s_ in range(L):
                if s_ // bs == t // bs and t >= mid and s_ < mid:
                    lmap[0, t, s_] = lvl
                    lmap[1, s_, t] = lvl
    w[:, (HG_LEVELS + 2) * L:, :] = 1
    return jnp.asarray(w, _BF16), jnp.asarray(lmap)


def _hgrn_kernel(q_ref, z_ref, v_ref, lb_ref, w_ref, lmap_ref, o_ref, st_ref):
    L = HG_CHUNK

    @pl.when(pl.program_id(2) == 0)
    def _():
        st_ref[...] = jnp.zeros_like(st_ref)

    w = w_ref[...]
    lmap = lmap_ref[...]
    for h in range(q_ref.shape[1] // HEAD_W):
        sl = slice(h * HEAD_W, (h + 1) * HEAD_W)
        z = z_ref[:, sl].astype(_F32)
        log_lb = lb_ref[0:1, sl]
        log_1m = lb_ref[1:2, sl]
        one_m = lb_ref[2:3, sl]
        ls = jnp.minimum(z, 0.0) - jnp.log1p(jnp.exp(-jnp.abs(z)))
        bb = log_1m + ls
        lf = jnp.maximum(log_lb, bb) + jnp.log1p(jnp.exp(-jnp.abs(log_lb - bb)))
        k = one_m * jnp.exp(ls - z)
        lf_hi = lf.astype(_BF16)
        lf_lo = (lf - lf_hi.astype(_F32)).astype(_BF16)
        e = jnp.exp(_dot(w, lf_hi) + _dot(w, lf_lo))
        q = q_ref[:, sl].astype(_F32)
        v = v_ref[:, sl]
        a = jnp.sum(q * k, axis=-1, keepdims=True) * jnp.ones((1, L), _F32)
        a = jnp.where(lmap == 0, a, 0.0)
        for lvl in range(1, HG_LEVELS + 1):
            el = e[(lvl + 1) * L:(lvl + 2) * L]
            pl_ = _dot_nt((q * el).astype(_BF16), (k * el).astype(_BF16))
            a = jnp.where(lmap == lvl, pl_, a)
        st = st_ref[h]
        o = _dot(a.astype(_BF16), v) + _dot_nt((q * e[0:L]).astype(_BF16), st.astype(_BF16))
        o_ref[:, sl] = o
        tot = e[(HG_LEVELS + 2) * L:(HG_LEVELS + 2) * L + 1]
        st_ref[h] = tot * st + _dot_tn(v, (k * e[L:2 * L]).astype(_BF16))


def _hgrn(p, lb_tab, w_tab, lmap, b, s, c, mix_w):
    n = p.shape[0]
    L = HG_CHUNK
    ncc, ncl = c // L, s // L
    nheads = mix_w // HEAD_W

    def row_block(bi, di, j):
        is_ctx = j < ncc
        cj = jnp.where(di == 0, j, ncc - 1 - j)
        lj = jnp.where(di == 0, j - ncc, ncl - 1 - (j - ncc))
        return jnp.where(is_ctx, (b * s + bi * c) // L + cj, bi * ncl + lj)

    def spec(col):
        return pl.BlockSpec((L, mix_w), lambda bi, di, j: (row_block(bi, di, j), col))

    return pl.pallas_call(
        _hgrn_kernel,
        grid=(b, 2, ncc + ncl),
        in_specs=[
            spec(0),
            pl.BlockSpec((L, mix_w), lambda bi, di, j: (row_block(bi, di, j), 1 + di)),
            spec(3),
            pl.BlockSpec((None, 3, mix_w), lambda bi, di, j: (di, 0, 0)),
            pl.BlockSpec((None,) + w_tab.shape[1:], lambda bi, di, j: (di, 0, 0)),
            pl.BlockSpec((None, L, L), lambda bi, di, j: (di, 0, 0)),
        ],
        out_specs=pl.BlockSpec((None, L, mix_w), lambda bi, di, j: (di, row_block(bi, di, j), 0)),
        out_shape=jax.ShapeDtypeStruct((2, n, mix_w), _F32),
        scratch_shapes=[pltpu.VMEM((nheads, HEAD_W, HEAD_W), _F32)],
        compiler_params=_params(("arbitrary", "arbitrary", "arbitrary")),
        name="hgrn_scan",
    )(p, p, p, lb_tab, w_tab, lmap)


def _attn_kernel(*refs, has_lat, tk, lam_init):
    if has_lat:
        q_ref, kc_ref, vc_ref, kl_ref, vl_ref, lamp_ref, nw_ref, o_ref, m_ref, l_ref, acc_ref = refs
    else:
        q_ref, kc_ref, vc_ref, lamp_ref, nw_ref, o_ref, m_ref, l_ref, acc_ref = refs
    tq = q_ref.shape[0]
    q = q_ref[...].astype(_F32) * (DA_HEAD_DIM ** -0.5)
    lane = lax.broadcasted_iota(jnp.int32, q.shape, 1)
    qq = jnp.concatenate([jnp.where(lane < DA_HEAD_DIM, q, 0.0), jnp.where(lane >= DA_HEAD_DIM, q, 0.0)],
                         axis=0).astype(_BF16)

    m_ref[...] = jnp.full_like(m_ref, -jnp.inf)
    l_ref[...] = jnp.zeros_like(l_ref)
    acc_ref[...] = jnp.zeros_like(acc_ref)

    def step(k, v):
        s = _dot_nt(qq, k)
        m_prev = m_ref[...]
        m_new = jnp.maximum(m_prev, jnp.max(s, axis=-1, keepdims=True))
        alpha = jnp.exp(m_prev - m_new)
        pexp = jnp.exp(s - m_new[:, 0:1])
        l_ref[...] = alpha * l_ref[...] + jnp.sum(pexp, axis=-1, keepdims=True)
        acc_ref[...] = alpha * acc_ref[...] + _dot(pexp.astype(_BF16), v)
        m_ref[...] = m_new

    step(kc_ref[...], vc_ref[...])
    if has_lat:
        def body(i, carry):
            off = pl.multiple_of(i * tk, tk)
            step(kl_ref[pl.ds(off, tk), :], vl_ref[pl.ds(off, tk), :])
            return carry
        lax.fori_loop(0, kl_ref.shape[0] // tk, body, 0)

    lp = lamp_ref[...]
    lam = (jnp.exp(jnp.sum(lp[0:1] * lp[1:2], axis=-1, keepdims=True))
           - jnp.exp(jnp.sum(lp[2:3] * lp[3:4], axis=-1, keepdims=True)) + lam_init)
    on = acc_ref[...] / l_ref[...]
    o = on[0:tq] - lam * on[tq:2 * tq]
    ms = jnp.mean(o * o, axis=-1, keepdims=True)
    o_ref[...] = (o * lax.rsqrt(ms + EPS) * nw_ref[...] * (1.0 - lam_init)).astype(o_ref.dtype)


def _attn(q_arr, q_col0, q_row0, tq, nq, p, kl_arr, b, s, c, mix_w, lam_p, nw, layer, lam_init, has_lat, tk=512):
    nheads = mix_w // HEAD_W
    ak0, av0 = 9 * mix_w // HEAD_W, 10 * mix_w // HEAD_W
    ctx_blk0 = b * s // c
    in_specs = [
        pl.BlockSpec((tq, HEAD_W), lambda bi, h, qi: (q_row0 // tq + bi * nq + qi, q_col0 // HEAD_W + h)),
        pl.BlockSpec((c, HEAD_W), lambda bi, h, qi: (ctx_blk0 + bi, ak0 + h)),
        pl.BlockSpec((c, HEAD_W), lambda bi, h, qi: (ctx_blk0 + bi, av0 + h)),
    ]
    args = [q_arr, p, p]
    if has_lat:
        in_specs += [
            pl.BlockSpec((s, HEAD_W), lambda bi, h, qi: (bi, nheads + h)),
            pl.BlockSpec((s, HEAD_W), lambda bi, h, qi: (bi, av0 + h)),
        ]
        args += [kl_arr, p]
    in_specs += [
        pl.BlockSpec((None, 4, DA_HEAD_DIM), lambda bi, h, qi: (layer, 0, 0)),
        pl.BlockSpec((None, 1, HEAD_W), lambda bi, h, qi: (layer, 0, 0)),
    ]
    args += [lam_p, nw.reshape(-1, 1, HEAD_W)]
    return pl.pallas_call(
        functools.partial(_attn_kernel, has_lat=has_lat, tk=min(tk, s), lam_init=lam_init),
        grid=(b, nheads, nq),
        in_specs=in_specs,
        out_specs=pl.BlockSpec((tq, HEAD_W), lambda bi, h, qi: (bi * nq + qi, h)),
        out_shape=jax.ShapeDtypeStruct((b * nq * tq, mix_w), _BF16),
        scratch_shapes=[pltpu.VMEM((2 * tq, HEAD_W), _F32), pltpu.VMEM((2 * tq, HEAD_W), _F32),
                        pltpu.VMEM((2 * tq, HEAD_W), _F32)],
        compiler_params=_params(("parallel", "parallel", "arbitrary")),
        name="diff_attn_lat" if has_lat else "diff_attn_ctx",
    )(*args)


def _branch_kernel(of_ref, ob_ref, hg_ref, cb_ref, cc_ref, cu_ref, ccp_ref, cup_ref, ccn_ref, cun_ref,
                   hnw_ref, cw_ref, a_ref, b_ref, *, n_lat, s, c):
    tm = of_ref.shape[0]
    nw = hnw_ref[...]
    for h in range(of_ref.shape[1] // HEAD_W):
        sl = slice(h * HEAD_W, (h + 1) * HEAD_W)
        o = of_ref[:, sl] + ob_ref[:, sl]
        ms = jnp.mean(o * o, axis=-1, keepdims=True)
        g = hg_ref[:, sl].astype(_F32)
        a_ref[:, sl] = ((o * lax.rsqrt(ms + EPS) * nw) * (g * jax.nn.sigmoid(g))).astype(a_ref.dtype)
    row = pl.program_id(0) * tm + lax.broadcasted_iota(jnp.int32, (tm, 1), 0)
    pos = jnp.where(row < n_lat, row % s, (row - n_lat) % c)
    seq_len = jnp.where(row < n_lat, s, c)
    local = lax.broadcasted_iota(jnp.int32, (tm, 1), 0)
    x = cc_ref[...].astype(_F32) * cu_ref[...].astype(_F32)
    x_halo_p = ccp_ref[7:8, :].astype(_F32) * cup_ref[7:8, :].astype(_F32)
    x_halo_n = ccn_ref[0:1, :].astype(_F32) * cun_ref[0:1, :].astype(_F32)
    x_prev = jnp.where(local == 0, x_halo_p, pltpu.roll(x, 1, axis=0))
    x_prev = jnp.where(pos == 0, 0.0, x_prev)
    x_next = jnp.where(local == tm - 1, x_halo_n, pltpu.roll(x, tm - 1, axis=0))
    x_next = jnp.where(pos == seq_len - 1, 0.0, x_next)
    y = cw_ref[0:1, :] * x_prev + cw_ref[1:2, :] * x + cw_ref[2:3, :] * x_next
    b_ref[...] = (cb_ref[...].astype(_F32) * y).astype(b_ref.dtype)


def _branches(o_fb, p, n_rows, hg_nw, conv_w, layer, b, s, c, mix_w, tm=256):
    tm = math.gcd(math.gcd(tm, s), b * c)
    last8 = p.shape[0] // 8 - 1
    main = lambda col: pl.BlockSpec((tm, mix_w), lambda i: (i, col))
    prev = lambda col: pl.BlockSpec((8, mix_w), lambda i: (jnp.maximum(i * (tm // 8) - 1, 0), col))
    nxt = lambda col: pl.BlockSpec((8, mix_w), lambda i: (jnp.minimum((i + 1) * (tm // 8), last8), col))
    out_sds = jax.ShapeDtypeStruct((n_rows, mix_w), _BF16)
    return pl.pallas_call(
        functools.partial(_branch_kernel, n_lat=b * s, s=s, c=c),
        grid=(n_rows // tm,),
        in_specs=[
            pl.BlockSpec((None, tm, mix_w), lambda i: (0, i, 0)),
            pl.BlockSpec((None, tm, mix_w), lambda i: (1, i, 0)),
            main(4), main(5), main(6), main(7), prev(6), prev(7), nxt(6), nxt(7),
            pl.BlockSpec((None, 1, HEAD_W), lambda i: (layer, 0, 0)),
            pl.BlockSpec((None, 3, mix_w), lambda i: (layer, 0, 0)),
        ],
        out_specs=[pl.BlockSpec((tm, mix_w), lambda i: (i, 0)), pl.BlockSpec((tm, mix_w), lambda i: (i, 0))],
        out_shape=[out_sds, out_sds],
        compiler_params=_params(("parallel",)),
        name="branch_prep",
    )(o_fb, o_fb, p, p, p, p, p, p, p, p, hg_nw.reshape(-1, 1, HEAD_W), conv_w)


def _merge_kernel(h_ref, mod_ref, a_ref, b_ref, c_ref, ga_ref, gb_ref, gc_ref, wa_ref, wb_ref, wc_ref,
                  wo_ref, o_ref):
    j = pl.program_id(1)

    @pl.when(j == 0)
    def _():
        o_ref[...] = jnp.zeros_like(o_ref)

    def gated(g_ref, x_ref, w_ref):
        return jax.nn.sigmoid(g_ref[...].astype(_F32)) * _dot(x_ref[...], w_ref[...])

    y = gated(ga_ref, a_ref, wa_ref) + gated(gb_ref, b_ref, wb_ref) + gated(gc_ref, c_ref, wc_ref)
    o_ref[...] += _dot(y.astype(_BF16), wo_ref[...])

    @pl.when(j == pl.num_programs(1) - 1)
    def _():
        o_ref[...] = h_ref[...] + mod_ref[5:6, :] * o_ref[...]


def _merge(h, n_rows, mod_l, a, bb, cc, p, w_branch, w_out, layer, dims, mix_w, tn=512):
    d = h.shape[1]
    tm = dims["tm"]
    g0 = 11 * mix_w // tn
    gstep = d // tn
    br = lambda: pl.BlockSpec((tm, mix_w), lambda i, j: (i, 0))
    gate = lambda k: pl.BlockSpec((tm, tn), lambda i, j: (i, g0 + k * gstep + j))
    wbr = lambda k: pl.BlockSpec((None, None, mix_w, tn), lambda i, j: (layer, k, 0, j))
    return pl.pallas_call(
        _merge_kernel,
        grid=(n_rows // tm, d // tn),
        in_specs=[
            pl.BlockSpec((tm, d), lambda i, j: (i, 0)),
            pl.BlockSpec((None, N_MOD, d), lambda i, j: (dims["mod_row"](i), 0, 0)),
            br(), br(), br(), gate(0), gate(1), gate(2), wbr(0), wbr(1), wbr(2),
            pl.BlockSpec((None, tn, d), lambda i, j: (layer, j, 0)),
        ],
        out_specs=pl.BlockSpec((tm, d), lambda i, j: (i, 0)),
        out_shape=jax.ShapeDtypeStruct((n_rows, d), _F32),
        compiler_params=_params(("parallel", "arbitrary")),
        name="merge_out",
    )(h, mod_l, a, bb, cc, p, p, p, w_branch, w_branch, w_branch, w_out)


def kernel(x, c, ctx, c_ctx, w_ada, b_ada, norm_w, ffn_w13, ffn_w2, w_in, hgrn_lb, hgrn_norm_w, conv_w,
           da_lambda, da_norm_w, w_branch, w_out, final_norm_w):
    b, s, d = x.shape
    cl = ctx.shape[1]
    depth = w_ada.shape[0]
    mix_w = d // 2
    n_lat, n_ctx = b * s, b * cl
    n_all = n_lat + n_ctx
    tm = math.gcd(math.gcd(512, s), n_ctx)
    dims = {
        "tm": tm,
        "mod_row": lambda i: jnp.minimum(i * tm // s, b),
        "norm_w": norm_w.reshape(depth * 3, 1, d),
    }

    w13_b, w2_b, w_in_b = ffn_w13.astype(_BF16), ffn_w2.astype(_BF16), w_in.astype(_BF16)
    w_branch_b, w_out_b = w_branch.astype(_BF16), w_out.astype(_BF16)

    cvec = jnp.concatenate([c, c_ctx[None, :], jnp.zeros((8 - b - 1, d), _F32)], axis=0)
    mod = _ada_mod(cvec, w_ada, b_ada).reshape(depth, 8, N_MOD, d)

    cum = jnp.cumsum(jax.nn.softmax(hgrn_lb.astype(_F32), axis=1), axis=1)
    lb = cum - cum[:, :1]
    lb_tab = jnp.stack([jnp.log(lb), jnp.log1p(-lb), 1.0 - lb], axis=2)
    w_tab, lmap = _hgrn_tables()
    cos, sin = _rope_tables(s)

    h = jnp.concatenate([x.reshape(n_lat, d), ctx.reshape(n_ctx, d)], axis=0)
    for layer in range(depth):
        last = layer == depth - 1
        lam_init = 0.8 - 0.6 * math.exp(-0.3 * layer)
        mod_l = mod[layer]
        h = _ffn(h, n_all, mod_l, layer * 3, w13_b, w2_b, layer, 0, 0, dims)
        p = _mixer_in(h, mod_l, layer * 3 + 1, w_in_b, layer, dims)
        qk = _rope(p, cos, sin, b, s, 8 * mix_w)
        o_fb = _hgrn(p, lb_tab[:, layer], w_tab, lmap, b, s, cl, mix_w)
        tq = min(512, s)
        c_att = _attn(qk, 0, 0, tq, s // tq, p, qk, b, s, cl, mix_w, da_lambda, da_norm_w, layer, lam_init, True)
        n_rows = n_lat if last else n_all
        if not last:
            c_ctx_att = _attn(p, 8 * mix_w, n_lat, cl, 1, p, None, b, s, cl, mix_w, da_lambda, da_norm_w,
                              layer, lam_init, False)
            c_att = jnp.concatenate([c_att, c_ctx_att], axis=0)
        a_br, b_br = _branches(o_fb, p, n_rows, hgrn_norm_w, conv_w, layer, b, s, cl, mix_w)
        h = _merge(h, n_rows, mod_l, a_br, b_br, c_att, p, w_branch_b, w_out_b, layer, dims, mix_w)
        h = _ffn(h, n_rows, mod_l, layer * 3 + 2, w13_b, w2_b, layer, 1, 6, dims,
                 final_w=final_norm_w if last else None)
    return h.reshape(b, s, d)
```

```python
import functools
import math

import jax
import jax.numpy as jnp
import numpy as np
from jax import lax
from jax.experimental import pallas as pl
from jax.experimental.pallas import tpu as pltpu

_F32 = jnp.float32
_BF16 = jnp.bfloat16

EPS = 1e-6
N_MOD = 9
GRID_W = 64
ROPE_THETA = 10000.0
HEAD_W = 128
DA_HEAD_DIM = 64
ATT_Q_SCALE = math.log2(math.e) * DA_HEAD_DIM ** -0.5
ATT_Q_GROUP = 1024
ATT_ONES_ROWS = 16
HG_CHUNK = 64
HG_LEVELS = 6
V7X_VMEM_LIMIT = 56 * 1024 * 1024


def _dot(a, b):
    return jnp.dot(a, b, preferred_element_type=_F32)


def _dot_nt(a, b):
    return lax.dot_general(a, b, (((1,), (1,)), ((), ())), preferred_element_type=_F32)


def _dot_tn(a, b):
    return lax.dot_general(a, b, (((0,), (0,)), ((), ())), preferred_element_type=_F32)


def _params(semantics, vmem=V7X_VMEM_LIMIT):
    return pltpu.CompilerParams(dimension_semantics=semantics, vmem_limit_bytes=vmem)


def _norm_mod(h, nw, shift, scale):
    ms = jnp.mean(h * h, axis=-1, keepdims=True)
    return (h * lax.rsqrt(ms + EPS) * nw) * (1.0 + scale) + shift


def _ada_kernel(c_ref, w_ref, b_ref, o_ref):
    c = c_ref[...]
    s = (c * jax.nn.sigmoid(c)).astype(_BF16)
    o_ref[...] = _dot(s, w_ref[...].astype(_BF16)) + b_ref[...]


def _ada_mod(cvec, w_ada, b_ada, tn=1024):
    depth, d, nd = w_ada.shape
    rows = cvec.shape[0]
    return pl.pallas_call(
        _ada_kernel,
        grid=(depth, nd // tn),
        in_specs=[
            pl.BlockSpec((rows, d), lambda l, j: (0, 0)),
            pl.BlockSpec((None, d, tn), lambda l, j: (l, 0, j)),
            pl.BlockSpec((None, 1, tn), lambda l, j: (l, 0, j)),
        ],
        out_specs=pl.BlockSpec((None, rows, tn), lambda l, j: (l, 0, j)),
        out_shape=jax.ShapeDtypeStruct((depth, rows, nd), _F32),
        compiler_params=_params(("arbitrary", "arbitrary")),
        name="ada_mod",
    )(cvec, w_ada, b_ada.reshape(depth, 1, nd))


def _ffn_kernel(h_ref, mod_ref, nw_ref, w1_ref, w3_ref, w2_ref, *rest, mod_off, final_norm):
    if final_norm:
        fw_ref, o_ref, xn_ref = rest
    else:
        o_ref, xn_ref = rest
    j = pl.program_id(1)

    @pl.when(j == 0)
    def _():
        xn = _norm_mod(h_ref[...], nw_ref[...], mod_ref[mod_off:mod_off + 1, :],
                       mod_ref[mod_off + 1:mod_off + 2, :])
        xn_ref[...] = xn.astype(_BF16)
        o_ref[...] = jnp.zeros_like(o_ref)

    xn = xn_ref[...]
    g = _dot(xn, w1_ref[...])
    u = _dot(xn, w3_ref[...])
    a = (g * jax.nn.sigmoid(g) * u).astype(_BF16)
    o_ref[...] += _dot(a, w2_ref[...])

    @pl.when(j == pl.num_programs(1) - 1)
    def _():
        gate = mod_ref[mod_off + 2:mod_off + 3, :]
        out = h_ref[...] + 0.5 * gate * o_ref[...]
        if final_norm:
            ms = jnp.mean(out * out, axis=-1, keepdims=True)
            out = out * lax.rsqrt(ms + EPS) * fw_ref[...]
        o_ref[...] = out


def _ffn(h, n_rows, mod_l, nw, w13, w2, layer, which, mod_off, dims, final_w=None, tf=512):
    d = h.shape[1]
    f = w2.shape[2]
    tm = dims["tm"]
    nfb = f // tf
    in_specs = [
        pl.BlockSpec((tm, d), lambda i, j: (i, 0)),
        pl.BlockSpec((None, N_MOD, d), lambda i, j: (dims["mod_row"](i), 0, 0)),
        pl.BlockSpec((None, 1, d), lambda i, j: (nw, 0, 0)),
        pl.BlockSpec((None, None, d, tf), lambda i, j: (layer, which, 0, j)),
        pl.BlockSpec((None, None, d, tf), lambda i, j: (layer, which, 0, j + nfb)),
        pl.BlockSpec((None, None, tf, d), lambda i, j: (layer, which, j, 0)),
    ]
    args = [h, mod_l, dims["norm_w"], w13, w13, w2]
    if final_w is not None:
        in_specs.append(pl.BlockSpec((1, d), lambda i, j: (0, 0)))
        args.append(final_w.reshape(1, d))
    return pl.pallas_call(
        functools.partial(_ffn_kernel, mod_off=mod_off, final_norm=final_w is not None),
        grid=(n_rows // tm, nfb),
        in_specs=in_specs,
        out_specs=pl.BlockSpec((tm, d), lambda i, j: (i, 0)),
        out_shape=jax.ShapeDtypeStruct((n_rows, d), _F32),
        scratch_shapes=[pltpu.VMEM((tm, d), _BF16)],
        compiler_params=_params(("parallel", "arbitrary")),
        name="half_ffn",
    )(*args)


def _proj_kernel(h_ref, mod_ref, nw_ref, w_ref, o_ref, xn_ref):
    @pl.when(pl.program_id(1) == 0)
    def _():
        xn = _norm_mod(h_ref[...], nw_ref[...], mod_ref[3:4, :], mod_ref[4:5, :])
        xn_ref[...] = xn.astype(_BF16)

    o_ref[...] = _dot(xn_ref[...], w_ref[...]).astype(o_ref.dtype)


def _mixer_in(h, mod_l, nw, w_in, layer, dims, tn=1024):
    n, d = h.shape
    cols = w_in.shape[2]
    tm = dims["tm"]
    return pl.pallas_call(
        _proj_kernel,
        grid=(n // tm, cols // tn),
        in_specs=[
            pl.BlockSpec((tm, d), lambda i, j: (i, 0)),
            pl.BlockSpec((None, N_MOD, d), lambda i, j: (dims["mod_row"](i), 0, 0)),
            pl.BlockSpec((None, 1, d), lambda i, j: (nw, 0, 0)),
            pl.BlockSpec((None, d, tn), lambda i, j: (layer, 0, j)),
        ],
        out_specs=pl.BlockSpec((tm, tn), lambda i, j: (i, j)),
        out_shape=jax.ShapeDtypeStruct((n, cols), _BF16),
        scratch_shapes=[pltpu.VMEM((tm, d), _BF16)],
        compiler_params=_params(("parallel", "arbitrary")),
        name="mixer_in",
    )(h, mod_l, dims["norm_w"], w_in)


def _rope_kernel(x_ref, cos_ref, sin_ref, o_ref):
    cos = cos_ref[...]
    sin = sin_ref[...]
    lane = lax.broadcasted_iota(jnp.int32, cos.shape, 1)
    first = (lane % 32) < 16
    n_groups = x_ref.shape[1] // HEAD_W
    for g in range(n_groups):
        sl = slice(g * HEAD_W, (g + 1) * HEAD_W)
        x = x_ref[:, sl].astype(_F32)
        rot = jnp.where(first, pltpu.roll(x, HEAD_W - 16, axis=1), pltpu.roll(x, 16, axis=1))
        y = x * cos + rot * sin
        if g < n_groups // 2:
            y = y * ATT_Q_SCALE
        o_ref[:, sl] = y.astype(o_ref.dtype)


def _rope_tables(n_tokens):
    rows = n_tokens // GRID_W
    row = jnp.repeat(jnp.arange(rows, dtype=jnp.int32), GRID_W)
    col = jnp.tile(jnp.arange(GRID_W, dtype=jnp.int32), rows)
    half = DA_HEAD_DIM // 2
    inv_freq = ROPE_THETA ** (-jnp.arange(0, half, 2, dtype=_F32) / half)
    ang_r = row.astype(_F32)[:, None] * inv_freq
    ang_c = col.astype(_F32)[:, None] * inv_freq
    ang = jnp.concatenate([ang_r, ang_r, ang_c, ang_c], axis=-1)
    ang = jnp.concatenate([ang, ang], axis=-1)
    sign = jnp.where((jnp.arange(HEAD_W) % 32) < 16, -1.0, 1.0)
    return jnp.cos(ang), jnp.sin(ang) * sign


def _rope(p, cos, sin, b, s, col0, tr=512):
    width = 2 * dims_mix_w(p)
    tr = min(tr, s)
    return pl.pallas_call(
        _rope_kernel,
        grid=(b * s // tr,),
        in_specs=[
            pl.BlockSpec((tr, width), lambda i: (i, col0 // width)),
            pl.BlockSpec((tr, HEAD_W), lambda i: (i % (s // tr), 0)),
            pl.BlockSpec((tr, HEAD_W), lambda i: (i % (s // tr), 0)),
        ],
        out_specs=pl.BlockSpec((tr, width), lambda i: (i, 0)),
        out_shape=jax.ShapeDtypeStruct((b * s, width), _BF16),
        compiler_params=_params(("parallel",)),
        name="axial_rope",
    )(p, cos, sin)


def dims_mix_w(p):
    return p.shape[1] // 17


def _hgrn_tables():
    L = HG_CHUNK
    w = np.zeros((2, (HG_LEVELS + 2) * L + 8, L), np.float32)
    lmap = np.full((2, L, L), HG_LEVELS + 1, np.int32)
    r = np.arange(L)
    for t in range(L):
        w[0, t, r <= t] = 1
        w[0, L + t, r > t] = 1
        w[1, t, r >= t] = 1
        w[1, L + t, r < t] = 1
        lmap[:, t, t] = 0
        for lvl in range(1, HG_LEVELS + 1):
            bs = (2 * L) >> lvl
            mid = (t // bs) * bs + bs // 2
            base = (lvl + 1) * L + t
            if t >= mid:
                w[0, base, (r >= mid) & (r <= t)] = 1
                w[1, base, (r >= mid) & (r < t)] = 1
            else:
                w[0, base, (r > t) & (r < mid)] = 1
                w[1, base, (r >= t) & (r < mid)] = 1
            for s_ in range(L):
                if s_ // bs == t // bs and t >= mid and s_ < mid:
                    lmap[0, t, s_] = lvl
                    lmap[1, s_, t] = lvl
    w[:, (HG_LEVELS + 2) * L:, :] = 1
    w = np.concatenate([w, w], axis=2)
    return jnp.asarray(w, _BF16), jnp.asarray(lmap)


def _hgrn_kernel(q_ref, z_ref, v_ref, lb_ref, w_ref, lmap_ref, o_ref, st_ref):
    L = HG_CHUNK

    @pl.when(pl.program_id(2) == 0)
    def _():
        st_ref[...] = jnp.zeros_like(st_ref)

    heads = [slice(h * HEAD_W, (h + 1) * HEAD_W) for h in range(q_ref.shape[1] // HEAD_W)]
    lmap = lmap_ref[...]
    z = z_ref[...].astype(_F32)
    ls = jnp.minimum(z, 0.0) - jnp.log1p(jnp.exp(-jnp.abs(z)))
    bb = lb_ref[1:2, :] + ls
    log_lb = lb_ref[0:1, :]
    lf = jnp.maximum(log_lb, bb) + jnp.log1p(jnp.exp(-jnp.abs(log_lb - bb)))
    k = lb_ref[2:3, :] * jnp.exp(ls - z)
    lf_hi = lf.astype(_BF16)
    lf_lo = (lf - lf_hi.astype(_F32)).astype(_BF16)
    e = jnp.exp(_dot(w_ref[...], jnp.concatenate([lf_hi, lf_lo], axis=0)))
    q = q_ref[...].astype(_F32)
    v = v_ref[...]
    qk = q * k
    a = [jnp.where(lmap == 0, jnp.sum(qk[:, sl], axis=-1, keepdims=True) * jnp.ones((1, L), _F32), 0.0)
         for sl in heads]
    for lvl in range(1, HG_LEVELS + 1):
        el = e[(lvl + 1) * L:(lvl + 2) * L]
        ql = (q * el).astype(_BF16)
        kl = (k * el).astype(_BF16)
        a = [jnp.where(lmap == lvl, _dot_nt(ql[:, sl], kl[:, sl]), a[h]) for h, sl in enumerate(heads)]
    q_in = (q * e[0:L]).astype(_BF16)
    k_out = (k * e[L:2 * L]).astype(_BF16)
    tot = e[(HG_LEVELS + 2) * L:(HG_LEVELS + 2) * L + 1]
    for h, sl in enumerate(heads):
        st = st_ref[h]
        o_ref[:, sl] = _dot(a[h].astype(_BF16), v[:, sl]) + _dot_nt(q_in[:, sl], st.astype(_BF16))
        st_ref[h] = tot[:, sl] * st + _dot_tn(v[:, sl], k_out[:, sl])


def _hgrn(p, lb_tab, w_tab, lmap, b, s, c, mix_w):
    n = p.shape[0]
    L = HG_CHUNK
    ncc, ncl = c // L, s // L
    nheads = mix_w // HEAD_W

    def row_block(bi, di, j):
        is_ctx = j < ncc
        cj = jnp.where(di == 0, j, ncc - 1 - j)
        lj = jnp.where(di == 0, j - ncc, ncl - 1 - (j - ncc))
        return jnp.where(is_ctx, (b * s + bi * c) // L + cj, bi * ncl + lj)

    def spec(col):
        return pl.BlockSpec((L, mix_w), lambda bi, di, j: (row_block(bi, di, j), col))

    return pl.pallas_call(
        _hgrn_kernel,
        grid=(b, 2, ncc + ncl),
        in_specs=[
            spec(0),
            pl.BlockSpec((L, mix_w), lambda bi, di, j: (row_block(bi, di, j), 1 + di)),
            spec(3),
            pl.BlockSpec((None, 3, mix_w), lambda bi, di, j: (di, 0, 0)),
            pl.BlockSpec((None,) + w_tab.shape[1:], lambda bi, di, j: (di, 0, 0)),
            pl.BlockSpec((None, L, L), lambda bi, di, j: (di, 0, 0)),
        ],
        out_specs=pl.BlockSpec((None, L, mix_w), lambda bi, di, j: (di, row_block(bi, di, j), 0)),
        out_shape=jax.ShapeDtypeStruct((2, n, mix_w), _F32),
        scratch_shapes=[pltpu.VMEM((nheads, HEAD_W, HEAD_W), _F32)],
        compiler_params=_params(("arbitrary", "arbitrary", "arbitrary")),
        name="hgrn_scan",
    )(p, p, p, lb_tab, w_tab, lmap)


def _attn_kernel(*refs, has_lat, tk, lam_init, q_scale):
    if has_lat:
        (q_ref, kc_ref, vc_ref, kl_ref, vl_ref, lamp_ref, nw_ref, o_ref, m_ref, acc_ref, vtc_ref, vtl_ref,
         st_ref) = refs
    else:
        q_ref, kc_ref, vc_ref, lamp_ref, nw_ref, o_ref, m_ref, acc_ref, vtc_ref = refs
    tq = q_ref.shape[0]
    eye = (lax.broadcasted_iota(jnp.int32, (HEAD_W, HEAD_W), 0)
           == lax.broadcasted_iota(jnp.int32, (HEAD_W, HEAD_W), 1)).astype(_F32).astype(_BF16)

    @pl.when(pl.program_id(2) == 0)
    def _():
        ones = jnp.ones((ATT_ONES_ROWS, vtc_ref.shape[1]), _BF16)
        vtc_ref[0:HEAD_W, :] = _dot_nt(eye, vc_ref[...]).astype(_BF16)
        vtc_ref[HEAD_W:, :] = ones
        if has_lat:
            def tbody(i, carry):
                off = pl.multiple_of(i * tk, tk)
                vtl_ref[0:HEAD_W, pl.ds(off, tk)] = _dot_nt(eye, vl_ref[pl.ds(off, tk), :]).astype(_BF16)
                vtl_ref[HEAD_W:, pl.ds(off, tk)] = jnp.ones((ATT_ONES_ROWS, tk), _BF16)
                return carry
            lax.fori_loop(0, vl_ref.shape[0] // tk, tbody, 0)

    q = q_ref[...].astype(_F32)
    if q_scale is not None:
        q = q * q_scale
    lane = lax.broadcasted_iota(jnp.int32, q.shape, 1)
    qq = jnp.concatenate([jnp.where(lane < DA_HEAD_DIM, q, 0.0), jnp.where(lane >= DA_HEAD_DIM, q, 0.0)],
                         axis=0).astype(_BF16)

    m_ref[...] = jnp.full_like(m_ref, -jnp.inf)
    acc_ref[...] = jnp.zeros_like(acc_ref)

    def softmax_pv(st, vt):
        m_prev = m_ref[...]
        m_new = jnp.maximum(m_prev, jnp.max(st, axis=0, keepdims=True))
        alpha = jnp.exp2(m_prev - m_new)
        pt = jnp.exp2(st - m_new).astype(_BF16)
        acc_ref[...] = alpha * acc_ref[...] + _dot(vt, pt)
        m_ref[...] = m_new

    def scores(t, buf):
        off = pl.multiple_of((t - 1) * tk, tk)
        st_ref[buf] = _dot_nt(kl_ref[pl.ds(off, tk), :], qq)

    st_c = _dot_nt(kc_ref[...], qq)
    if not has_lat:
        softmax_pv(st_c, vtc_ref[...])
    else:
        n = kl_ref.shape[0] // tk

        def consume(t, buf):
            off = pl.multiple_of((t - 1) * tk, tk)
            softmax_pv(st_ref[buf], vtl_ref[:, pl.ds(off, tk)])

        scores(1, 1)
        softmax_pv(st_c, vtc_ref[...])
        n_pairs = (n - 1) // 2

        def body(j, carry):
            t = 2 * j + 1
            scores(t + 1, 0)
            consume(t, 1)
            scores(t + 2, 1)
            consume(t + 1, 0)
            return carry
        lax.fori_loop(0, n_pairs, body, 0)
        for t in range(2 * n_pairs + 1, n + 1):
            if t < n:
                scores(t + 1, (t + 1) % 2)
            consume(t, t % 2)

    lp = lamp_ref[...]
    lam = (jnp.exp(jnp.sum(lp[0:1] * lp[1:2], axis=-1, keepdims=True))
           - jnp.exp(jnp.sum(lp[2:3] * lp[3:4], axis=-1, keepdims=True)) + lam_init)
    on = acc_ref[0:HEAD_W, :] / acc_ref[HEAD_W:HEAD_W + 1, :]
    ot = on[:, 0:tq] - lam * on[:, tq:2 * tq]
    ms = jnp.mean(ot * ot, axis=0, keepdims=True)
    o = (ot * lax.rsqrt(ms + EPS)).T
    o_ref[...] = (o * nw_ref[...] * (1.0 - lam_init)).astype(o_ref.dtype)


def _attn(q_arr, q_col0, q_row0, tq, nq, p, kl_arr, b, s, c, mix_w, lam_p, nw, layer, lam_init, has_lat, tk=1024):
    nheads = mix_w // HEAD_W
    ak0, av0 = 9 * mix_w // HEAD_W, 10 * mix_w // HEAD_W
    ctx_blk0 = b * s // c
    in_specs = [
        pl.BlockSpec((tq, HEAD_W), lambda bi, h, qi: (q_row0 // tq + bi * nq + qi, q_col0 // HEAD_W + h)),
        pl.BlockSpec((c, HEAD_W), lambda bi, h, qi: (ctx_blk0 + bi, ak0 + h)),
        pl.BlockSpec((c, HEAD_W), lambda bi, h, qi: (ctx_blk0 + bi, av0 + h)),
    ]
    args = [q_arr, p, p]
    if has_lat:
        in_specs += [
            pl.BlockSpec((s, HEAD_W), lambda bi, h, qi: (bi, nheads + h)),
            pl.BlockSpec((s, HEAD_W), lambda bi, h, qi: (bi, av0 + h)),
        ]
        args += [kl_arr, p]
    in_specs += [
        pl.BlockSpec((None, 4, DA_HEAD_DIM), lambda bi, h, qi: (layer, 0, 0)),
        pl.BlockSpec((None, 1, HEAD_W), lambda bi, h, qi: (layer, 0, 0)),
    ]
    args += [lam_p, nw.reshape(-1, 1, HEAD_W)]
    vt_rows = HEAD_W + ATT_ONES_ROWS
    scratch = [pltpu.VMEM((1, 2 * tq), _F32), pltpu.VMEM((vt_rows, 2 * tq), _F32), pltpu.VMEM((vt_rows, c), _BF16)]
    tk = min(tk, s)
    if has_lat:
        scratch += [pltpu.VMEM((vt_rows, s), _BF16), pltpu.VMEM((2, tk, 2 * tq), _F32)]
    return pl.pallas_call(
        functools.partial(_attn_kernel, has_lat=has_lat, tk=tk, lam_init=lam_init,
                          q_scale=None if has_lat else ATT_Q_SCALE),
        grid=(b, nheads, nq),
        in_specs=in_specs,
        out_specs=pl.BlockSpec((tq, HEAD_W), lambda bi, h, qi: (bi * nq + qi, h)),
        out_shape=jax.ShapeDtypeStruct((b * nq * tq, mix_w), _BF16),
        scratch_shapes=scratch,
        compiler_params=_params(("arbitrary", "arbitrary", "arbitrary")),
        name="diff_attn_lat" if has_lat else "diff_attn_ctx",
    )(*args)


def _branch_kernel(of_ref, ob_ref, hg_ref, cb_ref, cc_ref, cu_ref, ccp_ref, cup_ref, ccn_ref, cun_ref,
                   hnw_ref, cw_ref, a_ref, b_ref, *, n_lat, s, c):
    tm = of_ref.shape[0]
    nw = hnw_ref[...]
    for h in range(of_ref.shape[1] // HEAD_W):
        sl = slice(h * HEAD_W, (h + 1) * HEAD_W)
        o = of_ref[:, sl] + ob_ref[:, sl]
        ms = jnp.mean(o * o, axis=-1, keepdims=True)
        g = hg_ref[:, sl].astype(_F32)
        a_ref[:, sl] = ((o * lax.rsqrt(ms + EPS) * nw) * (g * jax.nn.sigmoid(g))).astype(a_ref.dtype)
    row = pl.program_id(0) * tm + lax.broadcasted_iota(jnp.int32, (tm, 1), 0)
    pos = jnp.where(row < n_lat, row % s, (row - n_lat) % c)
    seq_len = jnp.where(row < n_lat, s, c)
    local = lax.broadcasted_iota(jnp.int32, (tm, 1), 0)
    x = cc_ref[...].astype(_F32) * cu_ref[...].astype(_F32)
    x_halo_p = ccp_ref[7:8, :].astype(_F32) * cup_ref[7:8, :].astype(_F32)
    x_halo_n = ccn_ref[0:1, :].astype(_F32) * cun_ref[0:1, :].astype(_F32)
    x_prev = jnp.where(local == 0, x_halo_p, pltpu.roll(x, 1, axis=0))
    x_prev = jnp.where(pos == 0, 0.0, x_prev)
    x_next = jnp.where(local == tm - 1, x_halo_n, pltpu.roll(x, tm - 1, axis=0))
    x_next = jnp.where(pos == seq_len - 1, 0.0, x_next)
    y = cw_ref[0:1, :] * x_prev + cw_ref[1:2, :] * x + cw_ref[2:3, :] * x_next
    b_ref[...] = (cb_ref[...].astype(_F32) * y).astype(b_ref.dtype)


def _branches(o_fb, p, n_rows, hg_nw, conv_w, layer, b, s, c, mix_w, tm=256):
    tm = math.gcd(math.gcd(tm, s), b * c)
    last8 = p.shape[0] // 8 - 1
    main = lambda col: pl.BlockSpec((tm, mix_w), lambda i: (i, col))
    prev = lambda col: pl.BlockSpec((8, mix_w), lambda i: (jnp.maximum(i * (tm // 8) - 1, 0), col))
    nxt = lambda col: pl.BlockSpec((8, mix_w), lambda i: (jnp.minimum((i + 1) * (tm // 8), last8), col))
    out_sds = jax.ShapeDtypeStruct((n_rows, mix_w), _BF16)
    return pl.pallas_call(
        functools.partial(_branch_kernel, n_lat=b * s, s=s, c=c),
        grid=(n_rows // tm,),
        in_specs=[
            pl.BlockSpec((None, tm, mix_w), lambda i: (0, i, 0)),
            pl.BlockSpec((None, tm, mix_w), lambda i: (1, i, 0)),
            main(4), main(5), main(6), main(7), prev(6), prev(7), nxt(6), nxt(7),
            pl.BlockSpec((None, 1, HEAD_W), lambda i: (layer, 0, 0)),
            pl.BlockSpec((None, 3, mix_w), lambda i: (layer, 0, 0)),
        ],
        out_specs=[pl.BlockSpec((tm, mix_w), lambda i: (i, 0)), pl.BlockSpec((tm, mix_w), lambda i: (i, 0))],
        out_shape=[out_sds, out_sds],
        compiler_params=_params(("parallel",)),
        name="branch_prep",
    )(o_fb, o_fb, p, p, p, p, p, p, p, p, hg_nw.reshape(-1, 1, HEAD_W), conv_w)


def _merge_kernel(h_ref, mod_ref, a_ref, b_ref, c_ref, ga_ref, gb_ref, gc_ref, wa_ref, wb_ref, wc_ref,
                  wo_ref, o_ref):
    j = pl.program_id(1)

    @pl.when(j == 0)
    def _():
        o_ref[...] = jnp.zeros_like(o_ref)

    def gated(g_ref, x_ref, w_ref):
        return jax.nn.sigmoid(g_ref[...].astype(_F32)) * _dot(x_ref[...], w_ref[...])

    y = gated(ga_ref, a_ref, wa_ref) + gated(gb_ref, b_ref, wb_ref) + gated(gc_ref, c_ref, wc_ref)
    o_ref[...] += _dot(y.astype(_BF16), wo_ref[...])

    @pl.when(j == pl.num_programs(1) - 1)
    def _():
        o_ref[...] = h_ref[...] + mod_ref[5:6, :] * o_ref[...]


def _merge(h, n_rows, mod_l, a, bb, cc, p, w_branch, w_out, layer, dims, mix_w, tn=512):
    d = h.shape[1]
    tm = dims["tm"]
    g0 = 11 * mix_w // tn
    gstep = d // tn
    br = lambda: pl.BlockSpec((tm, mix_w), lambda i, j: (i, 0))
    gate = lambda k: pl.BlockSpec((tm, tn), lambda i, j: (i, g0 + k * gstep + j))
    wbr = lambda k: pl.BlockSpec((None, None, mix_w, tn), lambda i, j: (layer, k, 0, j))
    return pl.pallas_call(
        _merge_kernel,
        grid=(n_rows // tm, d // tn),
        in_specs=[
            pl.BlockSpec((tm, d), lambda i, j: (i, 0)),
            pl.BlockSpec((None, N_MOD, d), lambda i, j: (dims["mod_row"](i), 0, 0)),
            br(), br(), br(), gate(0), gate(1), gate(2), wbr(0), wbr(1), wbr(2),
            pl.BlockSpec((None, tn, d), lambda i, j: (layer, j, 0)),
        ],
        out_specs=pl.BlockSpec((tm, d), lambda i, j: (i, 0)),
        out_shape=jax.ShapeDtypeStruct((n_rows, d), _F32),
        compiler_params=_params(("parallel", "arbitrary")),
        name="merge_out",
    )(h, mod_l, a, bb, cc, p, p, p, w_branch, w_branch, w_branch, w_out)


def kernel(x, c, ctx, c_ctx, w_ada, b_ada, norm_w, ffn_w13, ffn_w2, w_in, hgrn_lb, hgrn_norm_w, conv_w,
           da_lambda, da_norm_w, w_branch, w_out, final_norm_w):
    b, s, d = x.shape
    cl = ctx.shape[1]
    depth = w_ada.shape[0]
    mix_w = d // 2
    n_lat, n_ctx = b * s, b * cl
    n_all = n_lat + n_ctx
    tm = math.gcd(math.gcd(512, s), n_ctx)
    dims = {
        "tm": tm,
        "mod_row": lambda i: jnp.minimum(i * tm // s, b),
        "norm_w": norm_w.reshape(depth * 3, 1, d),
    }

    w13_b, w2_b, w_in_b = ffn_w13.astype(_BF16), ffn_w2.astype(_BF16), w_in.astype(_BF16)
    w_branch_b, w_out_b = w_branch.astype(_BF16), w_out.astype(_BF16)

    cvec = jnp.concatenate([c, c_ctx[None, :], jnp.zeros((8 - b - 1, d), _F32)], axis=0)
    mod = _ada_mod(cvec, w_ada, b_ada).reshape(depth, 8, N_MOD, d)

    cum = jnp.cumsum(jax.nn.softmax(hgrn_lb.astype(_F32), axis=1), axis=1)
    lb = cum - cum[:, :1]
    lb_tab = jnp.stack([jnp.log(lb), jnp.log1p(-lb), 1.0 - lb], axis=2)
    w_tab, lmap = _hgrn_tables()
    cos, sin = _rope_tables(s)

    h = jnp.concatenate([x.reshape(n_lat, d), ctx.reshape(n_ctx, d)], axis=0)
    for layer in range(depth):
        last = layer == depth - 1
        lam_init = 0.8 - 0.6 * math.exp(-0.3 * layer)
        mod_l = mod[layer]
        h = _ffn(h, n_all, mod_l, layer * 3, w13_b, w2_b, layer, 0, 0, dims)
        p = _mixer_in(h, mod_l, layer * 3 + 1, w_in_b, layer, dims)
        qk = _rope(p, cos, sin, b, s, 8 * mix_w)
        o_fb = _hgrn(p, lb_tab[:, layer], w_tab, lmap, b, s, cl, mix_w)
        tq = min(512, s)
        c_att = _attn(qk, 0, 0, tq, s // tq, p, qk, b, s, cl, mix_w, da_lambda, da_norm_w, layer, lam_init, True)
        n_rows = n_lat if last else n_all
        if not last:
            c_ctx_att = _attn(p, 8 * mix_w, n_lat, cl, 1, p, None, b, s, cl, mix_w, da_lambda, da_norm_w,
                              layer, lam_init, False)
            c_att = jnp.concatenate([c_att, c_ctx_att], axis=0)
        a_br, b_br = _branches(o_fb, p, n_rows, hgrn_norm_w, conv_w, layer, b, s, cl, mix_w)
        h = _merge(h, n_rows, mod_l, a_br, b_br, c_att, p, w_branch_b, w_out_b, layer, dims, mix_w)
        h = _ffn(h, n_rows, mod_l, layer * 3 + 2, w13_b, w2_b, layer, 1, 6, dims,
                 final_w=final_norm_w if last else None)
    return h.reshape(b, s, d)
```

```python
import functools
import math

import jax
import jax.numpy as jnp
import numpy as np
from jax import lax
from jax.experimental import pallas as pl
from jax.experimental.pallas import tpu as pltpu

_F32 = jnp.float32
_BF16 = jnp.bfloat16

EPS = 1e-6
N_MOD = 9
GRID_W = 64
ROPE_THETA = 10000.0
HEAD_W = 128
DA_HEAD_DIM = 64
LOG2_E = math.log2(math.e)
ATT_Q_SCALE = LOG2_E * DA_HEAD_DIM ** -0.5
ATT_Q_GROUP = 1024
ATT_ONES_ROWS = 16
HG_CHUNK = 64
HG_LEVELS = 6
BF16_ROWS = 16
V7X_VMEM_LIMIT = 56 * 1024 * 1024


def _dot(a, b):
    return jnp.dot(a, b, preferred_element_type=_F32)


def _dot_nt(a, b):
    return lax.dot_general(a, b, (((1,), (1,)), ((), ())), preferred_element_type=_F32)


def _dot_tn(a, b):
    return lax.dot_general(a, b, (((0,), (0,)), ((), ())), preferred_element_type=_F32)


def _params(semantics, vmem=V7X_VMEM_LIMIT):
    return pltpu.CompilerParams(dimension_semantics=semantics, vmem_limit_bytes=vmem)


def _norm_mod(h, nw, shift, scale):
    ms = jnp.mean(h * h, axis=-1, keepdims=True)
    return (h * lax.rsqrt(ms + EPS) * nw) * (1.0 + scale) + shift


def _ada_kernel(c_ref, w_ref, b_ref, o_ref):
    c = c_ref[...]
    s = (c * jax.nn.sigmoid(c)).astype(_BF16)
    o_ref[...] = _dot(s, w_ref[...].astype(_BF16)) + b_ref[...]


def _ada_mod(cvec, w_ada, b_ada, tn=1024):
    depth, d, nd = w_ada.shape
    rows = cvec.shape[0]
    return pl.pallas_call(
        _ada_kernel,
        grid=(depth, nd // tn),
        in_specs=[
            pl.BlockSpec((rows, d), lambda l, j: (0, 0)),
            pl.BlockSpec((None, d, tn), lambda l, j: (l, 0, j)),
            pl.BlockSpec((None, 1, tn), lambda l, j: (l, 0, j)),
        ],
        out_specs=pl.BlockSpec((None, rows, tn), lambda l, j: (l, 0, j)),
        out_shape=jax.ShapeDtypeStruct((depth, rows, nd), _F32),
        compiler_params=_params(("arbitrary", "arbitrary")),
        name="ada_mod",
    )(cvec, w_ada, b_ada.reshape(depth, 1, nd))


def _stage_rows(n_rows, n_steps):
    return min(n_rows, -(-n_rows // (n_steps * BF16_ROWS)) * BF16_ROWS)


def _stage_xn(h_ref, mod_ref, nw_ref, xn_ref, slot, r0, rows, shift_row):
    xn = _norm_mod(h_ref[pl.ds(r0, rows), :], nw_ref[...], mod_ref[shift_row:shift_row + 1, :],
                   mod_ref[shift_row + 1:shift_row + 2, :])
    xn_ref[slot, pl.ds(r0, rows), :] = xn.astype(_BF16)


def _stage_next_tile(hn_ref, modn_ref, nw_ref, xn_ref, shift_row, rows):
    i, j = pl.program_id(0), pl.program_id(1)
    tm = hn_ref.shape[0]
    r0 = pl.multiple_of(jnp.minimum(j * rows, tm - rows), BF16_ROWS)
    _stage_xn(hn_ref, modn_ref, nw_ref, xn_ref, (i + 1) % 2, r0, rows, shift_row)


def _ffn_kernel(h_ref, hn_ref, mod_ref, modn_ref, nw_ref, w1_ref, w3_ref, w2_ref, *rest, mod_off, final_norm,
                stage_rows):
    if final_norm:
        fw_ref, o_ref, xn_ref = rest
    else:
        o_ref, xn_ref = rest
    i, j = pl.program_id(0), pl.program_id(1)

    @pl.when((i == 0) & (j == 0))
    def _():
        _stage_xn(h_ref, mod_ref, nw_ref, xn_ref, 0, 0, h_ref.shape[0], mod_off)

    @pl.when(j == 0)
    def _():
        o_ref[...] = jnp.zeros_like(o_ref)

    xn = xn_ref[i % 2]
    g = _dot(xn, w1_ref[...])
    u = _dot(xn, w3_ref[...])
    a = (g * jax.nn.sigmoid(g) * u).astype(_BF16)
    o_ref[...] += _dot(a, w2_ref[...])
    _stage_next_tile(hn_ref, modn_ref, nw_ref, xn_ref, mod_off, stage_rows)

    @pl.when(j == pl.num_programs(1) - 1)
    def _():
        gate = mod_ref[mod_off + 2:mod_off + 3, :]
        out = h_ref[...] + 0.5 * gate * o_ref[...]
        if final_norm:
            ms = jnp.mean(out * out, axis=-1, keepdims=True)
            out = out * lax.rsqrt(ms + EPS) * fw_ref[...]
        o_ref[...] = out


def _ffn(h, n_rows, mod_l, nw, w13, w2, layer, which, mod_off, dims, final_w=None, tf=512):
    d = h.shape[1]
    f = w2.shape[2]
    tm = dims["tm"]
    nfb = f // tf
    last = n_rows // tm - 1
    nxt = lambda i: jnp.minimum(i + 1, last)
    in_specs = [
        pl.BlockSpec((tm, d), lambda i, j: (i, 0)),
        pl.BlockSpec((tm, d), lambda i, j: (nxt(i), 0)),
        pl.BlockSpec((None, N_MOD, d), lambda i, j: (dims["mod_row"](i), 0, 0)),
        pl.BlockSpec((None, N_MOD, d), lambda i, j: (dims["mod_row"](nxt(i)), 0, 0)),
        pl.BlockSpec((None, 1, d), lambda i, j: (nw, 0, 0)),
        pl.BlockSpec((None, None, d, tf), lambda i, j: (layer, which, 0, j)),
        pl.BlockSpec((None, None, d, tf), lambda i, j: (layer, which, 0, j + nfb)),
        pl.BlockSpec((None, None, tf, d), lambda i, j: (layer, which, j, 0)),
    ]
    args = [h, h, mod_l, mod_l, dims["norm_w"], w13, w13, w2]
    if final_w is not None:
        in_specs.append(pl.BlockSpec((1, d), lambda i, j: (0, 0)))
        args.append(final_w.reshape(1, d))
    return pl.pallas_call(
        functools.partial(_ffn_kernel, mod_off=mod_off, final_norm=final_w is not None,
                          stage_rows=_stage_rows(tm, nfb)),
        grid=(n_rows // tm, nfb),
        in_specs=in_specs,
        out_specs=pl.BlockSpec((tm, d), lambda i, j: (i, 0)),
        out_shape=jax.ShapeDtypeStruct((n_rows, d), _F32),
        scratch_shapes=[pltpu.VMEM((2, tm, d), _BF16)],
        compiler_params=_params(("arbitrary", "arbitrary")),
        name="half_ffn",
    )(*args)


def _proj_kernel(h_ref, hn_ref, mod_ref, modn_ref, nw_ref, w_ref, o_ref, xn_ref, *, stage_rows):
    i, j = pl.program_id(0), pl.program_id(1)

    @pl.when((i == 0) & (j == 0))
    def _():
        _stage_xn(h_ref, mod_ref, nw_ref, xn_ref, 0, 0, h_ref.shape[0], 3)

    o_ref[...] = _dot(xn_ref[i % 2], w_ref[...]).astype(o_ref.dtype)
    _stage_next_tile(hn_ref, modn_ref, nw_ref, xn_ref, 3, stage_rows)


def _mixer_in(h, mod_l, nw, w_in, layer, dims, tn=1024):
    n, d = h.shape
    cols = w_in.shape[2]
    tm = dims["tm"]
    nxt = lambda i: jnp.minimum(i + 1, n // tm - 1)
    return pl.pallas_call(
        functools.partial(_proj_kernel, stage_rows=_stage_rows(tm, cols // tn)),
        grid=(n // tm, cols // tn),
        in_specs=[
            pl.BlockSpec((tm, d), lambda i, j: (i, 0)),
            pl.BlockSpec((tm, d), lambda i, j: (nxt(i), 0)),
            pl.BlockSpec((None, N_MOD, d), lambda i, j: (dims["mod_row"](i), 0, 0)),
            pl.BlockSpec((None, N_MOD, d), lambda i, j: (dims["mod_row"](nxt(i)), 0, 0)),
            pl.BlockSpec((None, 1, d), lambda i, j: (nw, 0, 0)),
            pl.BlockSpec((None, d, tn), lambda i, j: (layer, 0, j)),
        ],
        out_specs=pl.BlockSpec((tm, tn), lambda i, j: (i, j)),
        out_shape=jax.ShapeDtypeStruct((n, cols), _BF16),
        scratch_shapes=[pltpu.VMEM((2, tm, d), _BF16)],
        compiler_params=_params(("arbitrary", "arbitrary")),
        name="mixer_in",
    )(h, h, mod_l, mod_l, dims["norm_w"], w_in)


def _rope_kernel(x_ref, cos_ref, sin_ref, o_ref):
    cos = cos_ref[...]
    sin = sin_ref[...]
    lane = lax.broadcasted_iota(jnp.int32, cos.shape, 1)
    first = (lane % 32) < 16
    n_groups = x_ref.shape[1] // HEAD_W
    for g in range(n_groups):
        sl = slice(g * HEAD_W, (g + 1) * HEAD_W)
        x = x_ref[:, sl].astype(_F32)
        rot = jnp.where(first, pltpu.roll(x, HEAD_W - 16, axis=1), pltpu.roll(x, 16, axis=1))
        y = x * cos + rot * sin
        if g < n_groups // 2:
            y = y * ATT_Q_SCALE
        o_ref[:, sl] = y.astype(o_ref.dtype)


def _rope_tables(n_tokens):
    rows = n_tokens // GRID_W
    row = jnp.repeat(jnp.arange(rows, dtype=jnp.int32), GRID_W)
    col = jnp.tile(jnp.arange(GRID_W, dtype=jnp.int32), rows)
    half = DA_HEAD_DIM // 2
    inv_freq = ROPE_THETA ** (-jnp.arange(0, half, 2, dtype=_F32) / half)
    ang_r = row.astype(_F32)[:, None] * inv_freq
    ang_c = col.astype(_F32)[:, None] * inv_freq
    ang = jnp.concatenate([ang_r, ang_r, ang_c, ang_c], axis=-1)
    ang = jnp.concatenate([ang, ang], axis=-1)
    sign = jnp.where((jnp.arange(HEAD_W) % 32) < 16, -1.0, 1.0)
    return jnp.cos(ang), jnp.sin(ang) * sign


def _rope(p, cos, sin, b, s, col0, tr=512):
    width = 2 * dims_mix_w(p)
    tr = min(tr, s)
    return pl.pallas_call(
        _rope_kernel,
        grid=(b * s // tr,),
        in_specs=[
            pl.BlockSpec((tr, width), lambda i: (i, col0 // width)),
            pl.BlockSpec((tr, HEAD_W), lambda i: (i % (s // tr), 0)),
            pl.BlockSpec((tr, HEAD_W), lambda i: (i % (s // tr), 0)),
        ],
        out_specs=pl.BlockSpec((tr, width), lambda i: (i, 0)),
        out_shape=jax.ShapeDtypeStruct((b * s, width), _BF16),
        compiler_params=_params(("parallel",)),
        name="axial_rope",
    )(p, cos, sin)


def dims_mix_w(p):
    return p.shape[1] // 17


def _hgrn_tables():
    L = HG_CHUNK
    w = np.zeros((2, (HG_LEVELS + 2) * L + 8, L), np.float32)
    lmap = np.full((2, L, L), HG_LEVELS + 1, np.int32)
    r = np.arange(L)
    for t in range(L):
        w[0, t, r <= t] = 1
        w[0, L + t, r > t] = 1
        w[1, t, r >= t] = 1
        w[1, L + t, r < t] = 1
        lmap[:, t, t] = 0
        for lvl in range(1, HG_LEVELS + 1):
            bs = (2 * L) >> lvl
            mid = (t // bs) * bs + bs // 2
            base = (lvl + 1) * L + t
            if t >= mid:
                w[0, base, (r >= mid) & (r <= t)] = 1
                w[1, base, (r >= mid) & (r < t)] = 1
            else:
                w[0, base, (r > t) & (r < mid)] = 1
                w[1, base, (r >= t) & (r < mid)] = 1
            for s_ in range(L):
                if s_ // bs == t // bs and t >= mid and s_ < mid:
                    lmap[0, t, s_] = lvl
                    lmap[1, s_, t] = lvl
    w[:, (HG_LEVELS + 2) * L:, :] = 1
    w = np.concatenate([w, w], axis=2)
    return jnp.asarray(w, _BF16), jnp.asarray(lmap)


def _hgrn_kernel(q_ref, z_ref, v_ref, lb_ref, w_ref, lmap_ref, o_ref, st_ref):
    L = HG_CHUNK

    @pl.when(pl.program_id(2) == 0)
    def _():
        st_ref[...] = jnp.zeros_like(st_ref)

    heads = [slice(h * HEAD_W, (h + 1) * HEAD_W) for h in range(q_ref.shape[1] // HEAD_W)]
    lmap = lmap_ref[...]
    z = z_ref[...].astype(_F32)
    ls = jnp.minimum(z, 0.0) - jnp.log(1.0 + jnp.exp(-jnp.abs(z)))
    bb = lb_ref[1:2, :] + ls
    log_lb = lb_ref[0:1, :]
    lf = jnp.maximum(log_lb, bb) + jnp.log(1.0 + jnp.exp(-jnp.abs(log_lb - bb)))
    k = lb_ref[2:3, :] * jnp.exp(ls - z)
    lf2 = lf * LOG2_E
    lf_hi = lf2.astype(_BF16)
    lf_lo = (lf2 - lf_hi.astype(_F32)).astype(_BF16)
    e = jnp.exp2(_dot(w_ref[...], jnp.concatenate([lf_hi, lf_lo], axis=0)))
    q = q_ref[...].astype(_F32)
    v = v_ref[...]
    qk = q * k
    a = [jnp.where(lmap == 0, jnp.sum(qk[:, sl], axis=-1, keepdims=True) * jnp.ones((1, L), _F32), 0.0)
         for sl in heads]
    for lvl in range(1, HG_LEVELS + 1):
        el = e[(lvl + 1) * L:(lvl + 2) * L]
        ql = (q * el).astype(_BF16)
        kl = (k * el).astype(_BF16)
        a = [jnp.where(lmap == lvl, _dot_nt(ql[:, sl], kl[:, sl]), a[h]) for h, sl in enumerate(heads)]
    q_in = (q * e[0:L]).astype(_BF16)
    k_out = (k * e[L:2 * L]).astype(_BF16)
    tot = e[(HG_LEVELS + 2) * L:(HG_LEVELS + 2) * L + 1]
    for h, sl in enumerate(heads):
        st = st_ref[h]
        o_ref[:, sl] = _dot(a[h].astype(_BF16), v[:, sl]) + _dot_nt(q_in[:, sl], st.astype(_BF16))
        st_ref[h] = tot[:, sl] * st + _dot_tn(v[:, sl], k_out[:, sl])


def _hgrn(p, lb_tab, w_tab, lmap, b, s, c, mix_w):
    n = p.shape[0]
    L = HG_CHUNK
    ncc, ncl = c // L, s // L
    nheads = mix_w // HEAD_W

    def row_block(bi, di, j):
        is_ctx = j < ncc
        cj = jnp.where(di == 0, j, ncc - 1 - j)
        lj = jnp.where(di == 0, j - ncc, ncl - 1 - (j - ncc))
        return jnp.where(is_ctx, (b * s + bi * c) // L + cj, bi * ncl + lj)

    def spec(col):
        return pl.BlockSpec((L, mix_w), lambda bi, di, j: (row_block(bi, di, j), col))

    return pl.pallas_call(
        _hgrn_kernel,
        grid=(b, 2, ncc + ncl),
        in_specs=[
            spec(0),
            pl.BlockSpec((L, mix_w), lambda bi, di, j: (row_block(bi, di, j), 1 + di)),
            spec(3),
            pl.BlockSpec((None, 3, mix_w), lambda bi, di, j: (di, 0, 0)),
            pl.BlockSpec((None,) + w_tab.shape[1:], lambda bi, di, j: (di, 0, 0)),
            pl.BlockSpec((None, L, L), lambda bi, di, j: (di, 0, 0)),
        ],
        out_specs=pl.BlockSpec((None, L, mix_w), lambda bi, di, j: (di, row_block(bi, di, j), 0)),
        out_shape=jax.ShapeDtypeStruct((2, n, mix_w), _F32),
        scratch_shapes=[pltpu.VMEM((nheads, HEAD_W, HEAD_W), _F32)],
        compiler_params=_params(("arbitrary", "arbitrary", "arbitrary")),
        name="hgrn_scan",
    )(p, p, p, lb_tab, w_tab, lmap)


def _attn_kernel(*refs, has_lat, tk, lam_init, q_scale):
    if has_lat:
        (q_ref, kc_ref, vc_ref, kl_ref, vl_ref, lamp_ref, nw_ref, o_ref, m_ref, acc_ref, vtc_ref, vtl_ref,
         st_ref, mx_ref) = refs
    else:
        q_ref, kc_ref, vc_ref, lamp_ref, nw_ref, o_ref, m_ref, acc_ref, vtc_ref = refs
    tq = q_ref.shape[0]
    eye = (lax.broadcasted_iota(jnp.int32, (HEAD_W, HEAD_W), 0)
           == lax.broadcasted_iota(jnp.int32, (HEAD_W, HEAD_W), 1)).astype(_F32).astype(_BF16)

    @pl.when(pl.program_id(2) == 0)
    def _():
        ones = jnp.ones((ATT_ONES_ROWS, vtc_ref.shape[1]), _BF16)
        vtc_ref[0:HEAD_W, :] = _dot_nt(eye, vc_ref[...]).astype(_BF16)
        vtc_ref[HEAD_W:, :] = ones
        if has_lat:
            def tbody(i, carry):
                off = pl.multiple_of(i * tk, tk)
                vtl_ref[0:HEAD_W, pl.ds(off, tk)] = _dot_nt(eye, vl_ref[pl.ds(off, tk), :]).astype(_BF16)
                vtl_ref[HEAD_W:, pl.ds(off, tk)] = jnp.ones((ATT_ONES_ROWS, tk), _BF16)
                return carry
            lax.fori_loop(0, vl_ref.shape[0] // tk, tbody, 0)

    q = q_ref[...].astype(_F32)
    if q_scale is not None:
        q = q * q_scale
    lane = lax.broadcasted_iota(jnp.int32, q.shape, 1)
    qq = jnp.concatenate([jnp.where(lane < DA_HEAD_DIM, q, 0.0), jnp.where(lane >= DA_HEAD_DIM, q, 0.0)],
                         axis=0).astype(_BF16)

    m_ref[...] = jnp.full_like(m_ref, -jnp.inf)
    acc_ref[...] = jnp.zeros_like(acc_ref)

    def softmax_pv(st, st_max, vt):
        m_prev = m_ref[...]
        m_new = jnp.maximum(m_prev, st_max)
        alpha = jnp.exp2(m_prev - m_new)
        pt = jnp.exp2(st - m_new).astype(_BF16)
        acc_ref[...] = alpha * acc_ref[...] + _dot(vt, pt)
        m_ref[...] = m_new

    def scores(t, buf):
        off = pl.multiple_of((t - 1) * tk, tk)
        st = _dot_nt(kl_ref[pl.ds(off, tk), :], qq)
        st_ref[buf] = st
        mx_ref[buf] = jnp.max(st, axis=0, keepdims=True)

    st_c = _dot_nt(kc_ref[...], qq)
    mx_c = jnp.max(st_c, axis=0, keepdims=True)
    if not has_lat:
        softmax_pv(st_c, mx_c, vtc_ref[...])
    else:
        n = kl_ref.shape[0] // tk

        def consume(t, buf):
            off = pl.multiple_of((t - 1) * tk, tk)
            softmax_pv(st_ref[buf], mx_ref[buf], vtl_ref[:, pl.ds(off, tk)])

        scores(1, 1)
        softmax_pv(st_c, mx_c, vtc_ref[...])
        n_pairs = (n - 1) // 2

        def body(j, carry):
            t = 2 * j + 1
            scores(t + 1, 0)
            consume(t, 1)
            scores(t + 2, 1)
            consume(t + 1, 0)
            return carry
        lax.fori_loop(0, n_pairs, body, 0)
        for t in range(2 * n_pairs + 1, n + 1):
            if t < n:
                scores(t + 1, (t + 1) % 2)
            consume(t, t % 2)

    lp = lamp_ref[...]
    lam = (jnp.exp(jnp.sum(lp[0:1] * lp[1:2], axis=-1, keepdims=True))
           - jnp.exp(jnp.sum(lp[2:3] * lp[3:4], axis=-1, keepdims=True)) + lam_init)
    on = acc_ref[0:HEAD_W, :] / acc_ref[HEAD_W:HEAD_W + 1, :]
    ot = on[:, 0:tq] - lam * on[:, tq:2 * tq]
    ms = jnp.mean(ot * ot, axis=0, keepdims=True)
    o = (ot * lax.rsqrt(ms + EPS)).T
    o_ref[...] = (o * nw_ref[...] * (1.0 - lam_init)).astype(o_ref.dtype)


def _attn(q_arr, q_col0, q_row0, tq, nq, p, kl_arr, b, s, c, mix_w, lam_p, nw, layer, lam_init, has_lat, tk=1024):
    nheads = mix_w // HEAD_W
    ak0, av0 = 9 * mix_w // HEAD_W, 10 * mix_w // HEAD_W
    ctx_blk0 = b * s // c
    in_specs = [
        pl.BlockSpec((tq, HEAD_W), lambda bi, h, qi: (q_row0 // tq + bi * nq + qi, q_col0 // HEAD_W + h)),
        pl.BlockSpec((c, HEAD_W), lambda bi, h, qi: (ctx_blk0 + bi, ak0 + h)),
        pl.BlockSpec((c, HEAD_W), lambda bi, h, qi: (ctx_blk0 + bi, av0 + h)),
    ]
    args = [q_arr, p, p]
    if has_lat:
        in_specs += [
            pl.BlockSpec((s, HEAD_W), lambda bi, h, qi: (bi, nheads + h)),
            pl.BlockSpec((s, HEAD_W), lambda bi, h, qi: (bi, av0 + h)),
        ]
        args += [kl_arr, p]
    in_specs += [
        pl.BlockSpec((None, 4, DA_HEAD_DIM), lambda bi, h, qi: (layer, 0, 0)),
        pl.BlockSpec((None, 1, HEAD_W), lambda bi, h, qi: (layer, 0, 0)),
    ]
    args += [lam_p, nw.reshape(-1, 1, HEAD_W)]
    vt_rows = HEAD_W + ATT_ONES_ROWS
    scratch = [pltpu.VMEM((1, 2 * tq), _F32), pltpu.VMEM((vt_rows, 2 * tq), _F32), pltpu.VMEM((vt_rows, c), _BF16)]
    tk = min(tk, s)
    if has_lat:
        scratch += [pltpu.VMEM((vt_rows, s), _BF16), pltpu.VMEM((2, tk, 2 * tq), _F32),
                    pltpu.VMEM((2, 1, 2 * tq), _F32)]
    return pl.pallas_call(
        functools.partial(_attn_kernel, has_lat=has_lat, tk=tk, lam_init=lam_init,
                          q_scale=None if has_lat else ATT_Q_SCALE),
        grid=(b, nheads, nq),
        in_specs=in_specs,
        out_specs=pl.BlockSpec((tq, HEAD_W), lambda bi, h, qi: (bi * nq + qi, h)),
        out_shape=jax.ShapeDtypeStruct((b * nq * tq, mix_w), _BF16),
        scratch_shapes=scratch,
        compiler_params=_params(("arbitrary", "arbitrary", "arbitrary")),
        name="diff_attn_lat" if has_lat else "diff_attn_ctx",
    )(*args)


def _branch_kernel(of_ref, ob_ref, hg_ref, cb_ref, cc_ref, cu_ref, ccp_ref, cup_ref, ccn_ref, cun_ref,
                   hnw_ref, cw_ref, a_ref, b_ref, *, n_lat, s, c):
    tm = of_ref.shape[0]
    nw = hnw_ref[...]
    for h in range(of_ref.shape[1] // HEAD_W):
        sl = slice(h * HEAD_W, (h + 1) * HEAD_W)
        o = of_ref[:, sl] + ob_ref[:, sl]
        ms = jnp.mean(o * o, axis=-1, keepdims=True)
        g = hg_ref[:, sl].astype(_F32)
        a_ref[:, sl] = ((o * lax.rsqrt(ms + EPS) * nw) * (g * jax.nn.sigmoid(g))).astype(a_ref.dtype)
    row = pl.program_id(0) * tm + lax.broadcasted_iota(jnp.int32, (tm, 1), 0)
    pos = jnp.where(row < n_lat, row % s, (row - n_lat) % c)
    seq_len = jnp.where(row < n_lat, s, c)
    local = lax.broadcasted_iota(jnp.int32, (tm, 1), 0)
    x = cc_ref[...].astype(_F32) * cu_ref[...].astype(_F32)
    x_halo_p = ccp_ref[7:8, :].astype(_F32) * cup_ref[7:8, :].astype(_F32)
    x_halo_n = ccn_ref[0:1, :].astype(_F32) * cun_ref[0:1, :].astype(_F32)
    x_prev = jnp.where(local == 0, x_halo_p, pltpu.roll(x, 1, axis=0))
    x_prev = jnp.where(pos == 0, 0.0, x_prev)
    x_next = jnp.where(local == tm - 1, x_halo_n, pltpu.roll(x, tm - 1, axis=0))
    x_next = jnp.where(pos == seq_len - 1, 0.0, x_next)
    y = cw_ref[0:1, :] * x_prev + cw_ref[1:2, :] * x + cw_ref[2:3, :] * x_next
    b_ref[...] = (cb_ref[...].astype(_F32) * y).astype(b_ref.dtype)


def _branches(o_fb, p, n_rows, hg_nw, conv_w, layer, b, s, c, mix_w, tm=256):
    tm = math.gcd(math.gcd(tm, s), b * c)
    last8 = p.shape[0] // 8 - 1
    main = lambda col: pl.BlockSpec((tm, mix_w), lambda i: (i, col))
    prev = lambda col: pl.BlockSpec((8, mix_w), lambda i: (jnp.maximum(i * (tm // 8) - 1, 0), col))
    nxt = lambda col: pl.BlockSpec((8, mix_w), lambda i: (jnp.minimum((i + 1) * (tm // 8), last8), col))
    out_sds = jax.ShapeDtypeStruct((n_rows, mix_w), _BF16)
    return pl.pallas_call(
        functools.partial(_branch_kernel, n_lat=b * s, s=s, c=c),
        grid=(n_rows // tm,),
        in_specs=[
            pl.BlockSpec((None, tm, mix_w), lambda i: (0, i, 0)),
            pl.BlockSpec((None, tm, mix_w), lambda i: (1, i, 0)),
            main(4), main(5), main(6), main(7), prev(6), prev(7), nxt(6), nxt(7),
            pl.BlockSpec((None, 1, HEAD_W), lambda i: (layer, 0, 0)),
            pl.BlockSpec((None, 3, mix_w), lambda i: (layer, 0, 0)),
        ],
        out_specs=[pl.BlockSpec((tm, mix_w), lambda i: (i, 0)), pl.BlockSpec((tm, mix_w), lambda i: (i, 0))],
        out_shape=[out_sds, out_sds],
        compiler_params=_params(("parallel",)),
        name="branch_prep",
    )(o_fb, o_fb, p, p, p, p, p, p, p, p, hg_nw.reshape(-1, 1, HEAD_W), conv_w)


def _merge_kernel(h_ref, mod_ref, a_ref, b_ref, c_ref, ga_ref, gb_ref, gc_ref, wa_ref, wb_ref, wc_ref,
                  wo_ref, o_ref):
    j = pl.program_id(1)

    @pl.when(j == 0)
    def _():
        o_ref[...] = jnp.zeros_like(o_ref)

    def gated(g_ref, x_ref, w_ref):
        return jax.nn.sigmoid(g_ref[...].astype(_F32)) * _dot(x_ref[...], w_ref[...])

    y = gated(ga_ref, a_ref, wa_ref) + gated(gb_ref, b_ref, wb_ref) + gated(gc_ref, c_ref, wc_ref)
    o_ref[...] += _dot(y.astype(_BF16), wo_ref[...])

    @pl.when(j == pl.num_programs(1) - 1)
    def _():
        o_ref[...] = h_ref[...] + mod_ref[5:6, :] * o_ref[...]


def _merge(h, n_rows, mod_l, a, bb, cc, p, w_branch, w_out, layer, dims, mix_w, tn=512):
    d = h.shape[1]
    tm = dims["tm"]
    g0 = 11 * mix_w // tn
    gstep = d // tn
    br = lambda: pl.BlockSpec((tm, mix_w), lambda i, j: (i, 0))
    gate = lambda k: pl.BlockSpec((tm, tn), lambda i, j: (i, g0 + k * gstep + j))
    wbr = lambda k: pl.BlockSpec((None, None, mix_w, tn), lambda i, j: (layer, k, 0, j))
    return pl.pallas_call(
        _merge_kernel,
        grid=(n_rows // tm, d // tn),
        in_specs=[
            pl.BlockSpec((tm, d), lambda i, j: (i, 0)),
            pl.BlockSpec((None, N_MOD, d), lambda i, j: (dims["mod_row"](i), 0, 0)),
            br(), br(), br(), gate(0), gate(1), gate(2), wbr(0), wbr(1), wbr(2),
            pl.BlockSpec((None, tn, d), lambda i, j: (layer, j, 0)),
        ],
        out_specs=pl.BlockSpec((tm, d), lambda i, j: (i, 0)),
        out_shape=jax.ShapeDtypeStruct((n_rows, d), _F32),
        compiler_params=_params(("parallel", "arbitrary")),
        name="merge_out",
    )(h, mod_l, a, bb, cc, p, p, p, w_branch, w_branch, w_branch, w_out)


def kernel(x, c, ctx, c_ctx, w_ada, b_ada, norm_w, ffn_w13, ffn_w2, w_in, hgrn_lb, hgrn_norm_w, conv_w,
           da_lambda, da_norm_w, w_branch, w_out, final_norm_w):
    b, s, d = x.shape
    cl = ctx.shape[1]
    depth = w_ada.shape[0]
    mix_w = d // 2
    n_lat, n_ctx = b * s, b * cl
    n_all = n_lat + n_ctx
    tm = math.gcd(math.gcd(512, s), n_ctx)
    dims = {
        "tm": tm,
        "mod_row": lambda i: jnp.minimum(i * tm // s, b),
        "norm_w": norm_w.reshape(depth * 3, 1, d),
    }

    w13_b, w2_b, w_in_b = ffn_w13.astype(_BF16), ffn_w2.astype(_BF16), w_in.astype(_BF16)
    w_branch_b, w_out_b = w_branch.astype(_BF16), w_out.astype(_BF16)

    cvec = jnp.concatenate([c, c_ctx[None, :], jnp.zeros((8 - b - 1, d), _F32)], axis=0)
    mod = _ada_mod(cvec, w_ada, b_ada).reshape(depth, 8, N_MOD, d)

    cum = jnp.cumsum(jax.nn.softmax(hgrn_lb.astype(_F32), axis=1), axis=1)
    lb = cum - cum[:, :1]
    lb_tab = jnp.stack([jnp.log(lb), jnp.log1p(-lb), 1.0 - lb], axis=2)
    w_tab, lmap = _hgrn_tables()
    cos, sin = _rope_tables(s)

    h = jnp.concatenate([x.reshape(n_lat, d), ctx.reshape(n_ctx, d)], axis=0)
    for layer in range(depth):
        last = layer == depth - 1
        lam_init = 0.8 - 0.6 * math.exp(-0.3 * layer)
        mod_l = mod[layer]
        h = _ffn(h, n_all, mod_l, layer * 3, w13_b, w2_b, layer, 0, 0, dims)
        p = _mixer_in(h, mod_l, layer * 3 + 1, w_in_b, layer, dims)
        qk = _rope(p, cos, sin, b, s, 8 * mix_w)
        o_fb = _hgrn(p, lb_tab[:, layer], w_tab, lmap, b, s, cl, mix_w)
        tq = min(512, s)
        c_att = _attn(qk, 0, 0, tq, s // tq, p, qk, b, s, cl, mix_w, da_lambda, da_norm_w, layer, lam_init, True)
        n_rows = n_lat if last else n_all
        if not last:
            c_ctx_att = _attn(p, 8 * mix_w, n_lat, cl, 1, p, None, b, s, cl, mix_w, da_lambda, da_norm_w,
                              layer, lam_init, False)
            c_att = jnp.concatenate([c_att, c_ctx_att], axis=0)
        a_br, b_br = _branches(o_fb, p, n_rows, hgrn_norm_w, conv_w, layer, b, s, cl, mix_w)
        h = _merge(h, n_rows, mod_l, a_br, b_br, c_att, p, w_branch_b, w_out_b, layer, dims, mix_w)
        h = _ffn(h, n_rows, mod_l, layer * 3 + 2, w13_b, w2_b, layer, 1, 6, dims,
                 final_w=final_norm_w if last else None)
    return h.reshape(b, s, d)
```

```python
import functools
import math

import jax
import jax.numpy as jnp
import numpy as np
from jax import lax
from jax.experimental import pallas as pl
from jax.experimental.pallas import tpu as pltpu

_F32 = jnp.float32
_BF16 = jnp.bfloat16

EPS = 1e-6
N_MOD = 9
GRID_W = 64
ROPE_THETA = 10000.0
HEAD_W = 128
DA_HEAD_DIM = 64
LOG2_E = math.log2(math.e)
ATT_Q_SCALE = LOG2_E * DA_HEAD_DIM ** -0.5
ATT_Q_GROUP = 1024
ATT_ONES_ROWS = 16
HG_CHUNK = 64
HG_LEVELS = 6
V7X_VMEM_LIMIT = 56 * 1024 * 1024


def _dot(a, b):
    return jnp.dot(a, b, preferred_element_type=_F32)


def _dot_nt(a, b):
    return lax.dot_general(a, b, (((1,), (1,)), ((), ())), preferred_element_type=_F32)


def _dot_tn(a, b):
    return lax.dot_general(a, b, (((0,), (0,)), ((), ())), preferred_element_type=_F32)


def _params(semantics, vmem=V7X_VMEM_LIMIT):
    return pltpu.CompilerParams(dimension_semantics=semantics, vmem_limit_bytes=vmem)


def _norm_mod(h, nw, shift, scale):
    ms = jnp.mean(h * h, axis=-1, keepdims=True)
    return (h * lax.rsqrt(ms + EPS) * nw) * (1.0 + scale) + shift


def _ada_kernel(c_ref, w_ref, b_ref, o_ref):
    c = c_ref[...]
    s = (c * jax.nn.sigmoid(c)).astype(_BF16)
    o_ref[...] = _dot(s, w_ref[...].astype(_BF16)) + b_ref[...]


def _ada_mod(cvec, w_ada, b_ada, tn=1024):
    depth, d, nd = w_ada.shape
    rows = cvec.shape[0]
    return pl.pallas_call(
        _ada_kernel,
        grid=(depth, nd // tn),
        in_specs=[
            pl.BlockSpec((rows, d), lambda l, j: (0, 0)),
            pl.BlockSpec((None, d, tn), lambda l, j: (l, 0, j)),
            pl.BlockSpec((None, 1, tn), lambda l, j: (l, 0, j)),
        ],
        out_specs=pl.BlockSpec((None, rows, tn), lambda l, j: (l, 0, j)),
        out_shape=jax.ShapeDtypeStruct((depth, rows, nd), _F32),
        compiler_params=_params(("arbitrary", "arbitrary")),
        name="ada_mod",
    )(cvec, w_ada, b_ada.reshape(depth, 1, nd))


def _ffn_kernel(h_ref, mod_ref, nw_ref, w1_ref, w3_ref, w2_ref, *rest, mod_off, final_norm):
    if final_norm:
        fw_ref, o_ref, xn_ref = rest
    else:
        o_ref, xn_ref = rest
    j = pl.program_id(1)

    @pl.when(j == 0)
    def _():
        xn = _norm_mod(h_ref[...], nw_ref[...], mod_ref[mod_off:mod_off + 1, :],
                       mod_ref[mod_off + 1:mod_off + 2, :])
        xn_ref[...] = xn.astype(_BF16)
        o_ref[...] = jnp.zeros_like(o_ref)

    xn = xn_ref[...]
    g = _dot(xn, w1_ref[...])
    u = _dot(xn, w3_ref[...])
    a = (g * jax.nn.sigmoid(g) * u).astype(_BF16)
    o_ref[...] += _dot(a, w2_ref[...])

    @pl.when(j == pl.num_programs(1) - 1)
    def _():
        gate = mod_ref[mod_off + 2:mod_off + 3, :]
        out = h_ref[...] + 0.5 * gate * o_ref[...]
        if final_norm:
            ms = jnp.mean(out * out, axis=-1, keepdims=True)
            out = out * lax.rsqrt(ms + EPS) * fw_ref[...]
        o_ref[...] = out


def _ffn(h, n_rows, mod_l, nw, w13, w2, layer, which, mod_off, dims, final_w=None, tf=512):
    d = h.shape[1]
    f = w2.shape[2]
    tm = dims["tm"]
    nfb = f // tf
    mod_row = dims["mod_row"]
    in_specs = [
        pl.BlockSpec((tm, d), lambda i, j: (i, 0)),
        pl.BlockSpec((None, N_MOD, d), lambda i, j: (mod_row(i), 0, 0)),
        pl.BlockSpec((None, 1, d), lambda i, j: (nw, 0, 0)),
        pl.BlockSpec((None, None, d, tf), lambda i, j: (layer, which, 0, j)),
        pl.BlockSpec((None, None, d, tf), lambda i, j: (layer, which, 0, j + nfb)),
        pl.BlockSpec((None, None, tf, d), lambda i, j: (layer, which, j, 0)),
    ]
    args = [h, mod_l, dims["norm_w"], w13, w13, w2]
    if final_w is not None:
        in_specs.append(pl.BlockSpec((1, d), lambda i, j: (0, 0)))
        args.append(final_w.reshape(1, d))
    return pl.pallas_call(
        functools.partial(_ffn_kernel, mod_off=mod_off, final_norm=final_w is not None),
        grid=(pl.cdiv(n_rows, tm), nfb),
        in_specs=in_specs,
        out_specs=pl.BlockSpec((tm, d), lambda i, j: (i, 0)),
        out_shape=jax.ShapeDtypeStruct((n_rows, d), _F32),
        scratch_shapes=[pltpu.VMEM((tm, d), _BF16)],
        compiler_params=_params(("parallel", "arbitrary")),
        name="half_ffn",
    )(*args)


def _proj_kernel(h_ref, mod_ref, nw_ref, w_ref, o_ref, xn_ref):
    @pl.when(pl.program_id(1) == 0)
    def _():
        xn = _norm_mod(h_ref[...], nw_ref[...], mod_ref[3:4, :], mod_ref[4:5, :])
        xn_ref[...] = xn.astype(_BF16)

    o_ref[...] = _dot(xn_ref[...], w_ref[...]).astype(o_ref.dtype)


def _mixer_in(h, mod_l, nw, w_in, layer, dims, tn=1024):
    n, d = h.shape
    cols = w_in.shape[2]
    tm = dims["tm_big"]
    mod_row = lambda i: jnp.minimum(i * tm // dims["s"], dims["b"])
    return pl.pallas_call(
        _proj_kernel,
        grid=(pl.cdiv(n, tm), cols // tn),
        in_specs=[
            pl.BlockSpec((tm, d), lambda i, j: (i, 0)),
            pl.BlockSpec((None, N_MOD, d), lambda i, j: (mod_row(i), 0, 0)),
            pl.BlockSpec((None, 1, d), lambda i, j: (nw, 0, 0)),
            pl.BlockSpec((None, d, tn), lambda i, j: (layer, 0, j)),
        ],
        out_specs=pl.BlockSpec((tm, tn), lambda i, j: (i, j)),
        out_shape=jax.ShapeDtypeStruct((n, cols), _BF16),
        scratch_shapes=[pltpu.VMEM((tm, d), _BF16)],
        compiler_params=_params(("parallel", "arbitrary")),
        name="mixer_in",
    )(h, mod_l, dims["norm_w"], w_in)


def _rope_kernel(x_ref, cos_ref, sin_ref, o_ref):
    cos = cos_ref[...]
    sin = sin_ref[...]
    lane = lax.broadcasted_iota(jnp.int32, cos.shape, 1)
    first = (lane % 32) < 16
    n_groups = x_ref.shape[1] // HEAD_W
    for g in range(n_groups):
        sl = slice(g * HEAD_W, (g + 1) * HEAD_W)
        x = x_ref[:, sl].astype(_F32)
        rot = jnp.where(first, pltpu.roll(x, HEAD_W - 16, axis=1), pltpu.roll(x, 16, axis=1))
        y = x * cos + rot * sin
        if g < n_groups // 2:
            y = y * ATT_Q_SCALE
        o_ref[:, sl] = y.astype(o_ref.dtype)


def _rope_tables(n_tokens):
    rows = n_tokens // GRID_W
    row = jnp.repeat(jnp.arange(rows, dtype=jnp.int32), GRID_W)
    col = jnp.tile(jnp.arange(GRID_W, dtype=jnp.int32), rows)
    half = DA_HEAD_DIM // 2
    inv_freq = ROPE_THETA ** (-jnp.arange(0, half, 2, dtype=_F32) / half)
    ang_r = row.astype(_F32)[:, None] * inv_freq
    ang_c = col.astype(_F32)[:, None] * inv_freq
    ang = jnp.concatenate([ang_r, ang_r, ang_c, ang_c], axis=-1)
    ang = jnp.concatenate([ang, ang], axis=-1)
    sign = jnp.where((jnp.arange(HEAD_W) % 32) < 16, -1.0, 1.0)
    return jnp.cos(ang), jnp.sin(ang) * sign


def _rope(p, cos, sin, b, s, col0, tr=512):
    width = 2 * dims_mix_w(p)
    tr = min(tr, s)
    return pl.pallas_call(
        _rope_kernel,
        grid=(b * s // tr,),
        in_specs=[
            pl.BlockSpec((tr, width), lambda i: (i, col0 // width)),
            pl.BlockSpec((tr, HEAD_W), lambda i: (i % (s // tr), 0)),
            pl.BlockSpec((tr, HEAD_W), lambda i: (i % (s // tr), 0)),
        ],
        out_specs=pl.BlockSpec((tr, width), lambda i: (i, 0)),
        out_shape=jax.ShapeDtypeStruct((b * s, width), _BF16),
        compiler_params=_params(("parallel",)),
        name="axial_rope",
    )(p, cos, sin)


def dims_mix_w(p):
    return p.shape[1] // 17


def _hgrn_tables():
    L = HG_CHUNK
    w = np.zeros((2, (HG_LEVELS + 2) * L + 8, L), np.float32)
    lmap = np.full((2, L, L), HG_LEVELS + 1, np.int32)
    r = np.arange(L)
    for t in range(L):
        w[0, t, r <= t] = 1
        w[0, L + t, r > t] = 1
        w[1, t, r >= t] = 1
        w[1, L + t, r < t] = 1
        lmap[:, t, t] = 0
        for lvl in range(1, HG_LEVELS + 1):
            bs = (2 * L) >> lvl
            mid = (t // bs) * bs + bs // 2
            base = (lvl + 1) * L + t
            if t >= mid:
                w[0, base, (r >= mid) & (r <= t)] = 1
                w[1, base, (r >= mid) & (r < t)] = 1
            else:
                w[0, base, (r > t) & (r < mid)] = 1
                w[1, base, (r >= t) & (r < mid)] = 1
            for s_ in range(L):
                if s_ // bs == t // bs and t >= mid and s_ < mid:
                    lmap[0, t, s_] = lvl
                    lmap[1, s_, t] = lvl
    w[:, (HG_LEVELS + 2) * L:, :] = 1
    w = np.concatenate([w, w], axis=2)
    return jnp.asarray(w, _BF16), jnp.asarray(lmap)


def _hgrn_kernel(q_ref, z_ref, v_ref, lb_ref, w_ref, lmap_ref, o_ref, st_ref):
    L = HG_CHUNK

    @pl.when(pl.program_id(2) == 0)
    def _():
        st_ref[...] = jnp.zeros_like(st_ref)

    heads = [slice(h * HEAD_W, (h + 1) * HEAD_W) for h in range(q_ref.shape[1] // HEAD_W)]
    lmap = lmap_ref[...]
    z = z_ref[...].astype(_F32)
    ls = jnp.minimum(z, 0.0) - jnp.log(1.0 + jnp.exp(-jnp.abs(z)))
    bb = lb_ref[1:2, :] + ls
    log_lb = lb_ref[0:1, :]
    lf = jnp.maximum(log_lb, bb) + jnp.log(1.0 + jnp.exp(-jnp.abs(log_lb - bb)))
    k = lb_ref[2:3, :] * jnp.exp(ls - z)
    lf2 = lf * LOG2_E
    lf_hi = lf2.astype(_BF16)
    lf_lo = (lf2 - lf_hi.astype(_F32)).astype(_BF16)
    e = jnp.exp2(_dot(w_ref[...], jnp.concatenate([lf_hi, lf_lo], axis=0)))
    q = q_ref[...].astype(_F32)
    v = v_ref[...]
    qk = q * k
    a = [jnp.where(lmap == 0, jnp.sum(qk[:, sl], axis=-1, keepdims=True) * jnp.ones((1, L), _F32), 0.0)
         for sl in heads]
    for lvl in range(1, HG_LEVELS + 1):
        el = e[(lvl + 1) * L:(lvl + 2) * L]
        ql = (q * el).astype(_BF16)
        kl = (k * el).astype(_BF16)
        a = [jnp.where(lmap == lvl, _dot_nt(ql[:, sl], kl[:, sl]), a[h]) for h, sl in enumerate(heads)]
    q_in = (q * e[0:L]).astype(_BF16)
    k_out = (k * e[L:2 * L]).astype(_BF16)
    tot = e[(HG_LEVELS + 2) * L:(HG_LEVELS + 2) * L + 1]
    for h, sl in enumerate(heads):
        st = st_ref[h]
        o_ref[:, sl] = _dot(a[h].astype(_BF16), v[:, sl]) + _dot_nt(q_in[:, sl], st.astype(_BF16))
        st_ref[h] = tot[:, sl] * st + _dot_tn(v[:, sl], k_out[:, sl])


def _hgrn(p, lb_tab, w_tab, lmap, b, s, c, mix_w):
    n = p.shape[0]
    L = HG_CHUNK
    ncc, ncl = c // L, s // L
    nheads = mix_w // HEAD_W

    def row_block(bi, di, j):
        is_ctx = j < ncc
        cj = jnp.where(di == 0, j, ncc - 1 - j)
        lj = jnp.where(di == 0, j - ncc, ncl - 1 - (j - ncc))
        return jnp.where(is_ctx, (b * s + bi * c) // L + cj, bi * ncl + lj)

    def spec(col):
        return pl.BlockSpec((L, mix_w), lambda bi, di, j: (row_block(bi, di, j), col))

    return pl.pallas_call(
        _hgrn_kernel,
        grid=(b, 2, ncc + ncl),
        in_specs=[
            spec(0),
            pl.BlockSpec((L, mix_w), lambda bi, di, j: (row_block(bi, di, j), 1 + di)),
            spec(3),
            pl.BlockSpec((None, 3, mix_w), lambda bi, di, j: (di, 0, 0)),
            pl.BlockSpec((None,) + w_tab.shape[1:], lambda bi, di, j: (di, 0, 0)),
            pl.BlockSpec((None, L, L), lambda bi, di, j: (di, 0, 0)),
        ],
        out_specs=pl.BlockSpec((None, L, mix_w), lambda bi, di, j: (di, row_block(bi, di, j), 0)),
        out_shape=jax.ShapeDtypeStruct((2, n, mix_w), _F32),
        scratch_shapes=[pltpu.VMEM((nheads, HEAD_W, HEAD_W), _F32)],
        compiler_params=_params(("arbitrary", "arbitrary", "arbitrary")),
        name="hgrn_scan",
    )(p, p, p, lb_tab, w_tab, lmap)


def _attn_kernel(*refs, has_lat, tk, lam_init, q_scale):
    if has_lat:
        (q_ref, kc_ref, vc_ref, kl_ref, vl_ref, lamp_ref, nw_ref, o_ref, m_ref, acc_ref, vtc_ref, vtl_ref,
         st_ref, mx_ref) = refs
    else:
        q_ref, kc_ref, vc_ref, lamp_ref, nw_ref, o_ref, m_ref, acc_ref, vtc_ref = refs
    tq = q_ref.shape[0]
    eye = (lax.broadcasted_iota(jnp.int32, (HEAD_W, HEAD_W), 0)
           == lax.broadcasted_iota(jnp.int32, (HEAD_W, HEAD_W), 1)).astype(_F32).astype(_BF16)

    @pl.when(pl.program_id(2) == 0)
    def _():
        ones = jnp.ones((ATT_ONES_ROWS, vtc_ref.shape[1]), _BF16)
        vtc_ref[0:HEAD_W, :] = _dot_nt(eye, vc_ref[...]).astype(_BF16)
        vtc_ref[HEAD_W:, :] = ones
        if has_lat:
            def tbody(i, carry):
                off = pl.multiple_of(i * tk, tk)
                vtl_ref[0:HEAD_W, pl.ds(off, tk)] = _dot_nt(eye, vl_ref[pl.ds(off, tk), :]).astype(_BF16)
                vtl_ref[HEAD_W:, pl.ds(off, tk)] = jnp.ones((ATT_ONES_ROWS, tk), _BF16)
                return carry
            lax.fori_loop(0, vl_ref.shape[0] // tk, tbody, 0)

    q = q_ref[...].astype(_F32)
    if q_scale is not None:
        q = q * q_scale
    lane = lax.broadcasted_iota(jnp.int32, q.shape, 1)
    qq = jnp.concatenate([jnp.where(lane < DA_HEAD_DIM, q, 0.0), jnp.where(lane >= DA_HEAD_DIM, q, 0.0)],
                         axis=0).astype(_BF16)

    m_ref[...] = jnp.full_like(m_ref, -jnp.inf)
    acc_ref[...] = jnp.zeros_like(acc_ref)

    def softmax_pv(st, st_max, vt):
        m_prev = m_ref[...]
        m_new = jnp.maximum(m_prev, st_max)
        alpha = jnp.exp2(m_prev - m_new)
        pt = jnp.exp2(st - m_new).astype(_BF16)
        acc_ref[...] = alpha * acc_ref[...] + _dot(vt, pt)
        m_ref[...] = m_new

    def scores(t, buf):
        off = pl.multiple_of((t - 1) * tk, tk)
        st = _dot_nt(kl_ref[pl.ds(off, tk), :], qq)
        st_ref[buf] = st
        mx_ref[buf] = jnp.max(st, axis=0, keepdims=True)

    st_c = _dot_nt(kc_ref[...], qq)
    mx_c = jnp.max(st_c, axis=0, keepdims=True)
    if not has_lat:
        softmax_pv(st_c, mx_c, vtc_ref[...])
    else:
        n = kl_ref.shape[0] // tk

        def consume(t, buf):
            off = pl.multiple_of((t - 1) * tk, tk)
            softmax_pv(st_ref[buf], mx_ref[buf], vtl_ref[:, pl.ds(off, tk)])

        scores(1, 1)
        softmax_pv(st_c, mx_c, vtc_ref[...])
        n_pairs = (n - 1) // 2

        def body(j, carry):
            t = 2 * j + 1
            scores(t + 1, 0)
            consume(t, 1)
            scores(t + 2, 1)
            consume(t + 1, 0)
            return carry
        lax.fori_loop(0, n_pairs, body, 0)
        for t in range(2 * n_pairs + 1, n + 1):
            if t < n:
                scores(t + 1, (t + 1) % 2)
            consume(t, t % 2)

    lp = lamp_ref[...]
    lam = (jnp.exp(jnp.sum(lp[0:1] * lp[1:2], axis=-1, keepdims=True))
           - jnp.exp(jnp.sum(lp[2:3] * lp[3:4], axis=-1, keepdims=True)) + lam_init)
    on = acc_ref[0:HEAD_W, :] / acc_ref[HEAD_W:HEAD_W + 1, :]
    ot = on[:, 0:tq] - lam * on[:, tq:2 * tq]
    ms = jnp.mean(ot * ot, axis=0, keepdims=True)
    o = (ot * lax.rsqrt(ms + EPS)).T
    o_ref[...] = (o * nw_ref[...] * (1.0 - lam_init)).astype(o_ref.dtype)


def _attn(q_arr, q_col0, q_row0, tq, nq, p, kl_arr, b, s, c, mix_w, lam_p, nw, layer, lam_init, has_lat, tk=1024):
    nheads = mix_w // HEAD_W
    ak0, av0 = 9 * mix_w // HEAD_W, 10 * mix_w // HEAD_W
    ctx_blk0 = b * s // c
    in_specs = [
        pl.BlockSpec((tq, HEAD_W), lambda bi, h, qi: (q_row0 // tq + bi * nq + qi, q_col0 // HEAD_W + h)),
        pl.BlockSpec((c, HEAD_W), lambda bi, h, qi: (ctx_blk0 + bi, ak0 + h)),
        pl.BlockSpec((c, HEAD_W), lambda bi, h, qi: (ctx_blk0 + bi, av0 + h)),
    ]
    args = [q_arr, p, p]
    if has_lat:
        in_specs += [
            pl.BlockSpec((s, HEAD_W), lambda bi, h, qi: (bi, nheads + h)),
            pl.BlockSpec((s, HEAD_W), lambda bi, h, qi: (bi, av0 + h)),
        ]
        args += [kl_arr, p]
    in_specs += [
        pl.BlockSpec((None, 4, DA_HEAD_DIM), lambda bi, h, qi: (layer, 0, 0)),
        pl.BlockSpec((None, 1, HEAD_W), lambda bi, h, qi: (layer, 0, 0)),
    ]
    args += [lam_p, nw.reshape(-1, 1, HEAD_W)]
    vt_rows = HEAD_W + ATT_ONES_ROWS
    scratch = [pltpu.VMEM((1, 2 * tq), _F32), pltpu.VMEM((vt_rows, 2 * tq), _F32), pltpu.VMEM((vt_rows, c), _BF16)]
    tk = min(tk, s)
    if has_lat:
        scratch += [pltpu.VMEM((vt_rows, s), _BF16), pltpu.VMEM((2, tk, 2 * tq), _F32),
                    pltpu.VMEM((2, 1, 2 * tq), _F32)]
    return pl.pallas_call(
        functools.partial(_attn_kernel, has_lat=has_lat, tk=tk, lam_init=lam_init,
                          q_scale=None if has_lat else ATT_Q_SCALE),
        grid=(b, nheads, nq),
        in_specs=in_specs,
        out_specs=pl.BlockSpec((tq, HEAD_W), lambda bi, h, qi: (bi * nq + qi, h)),
        out_shape=jax.ShapeDtypeStruct((b * nq * tq, mix_w), _BF16),
        scratch_shapes=scratch,
        compiler_params=_params(("arbitrary", "arbitrary", "arbitrary")),
        name="diff_attn_lat" if has_lat else "diff_attn_ctx",
    )(*args)


def _branch_kernel(of_ref, ob_ref, hg_ref, cb_ref, cc_ref, cu_ref, ccp_ref, cup_ref, ccn_ref, cun_ref,
                   hnw_ref, cw_ref, a_ref, b_ref, *, n_lat, s, c):
    tm = of_ref.shape[0]
    nw = hnw_ref[...]
    for h in range(of_ref.shape[1] // HEAD_W):
        sl = slice(h * HEAD_W, (h + 1) * HEAD_W)
        o = of_ref[:, sl] + ob_ref[:, sl]
        ms = jnp.mean(o * o, axis=-1, keepdims=True)
        g = hg_ref[:, sl].astype(_F32)
        a_ref[:, sl] = ((o * lax.rsqrt(ms + EPS) * nw) * (g * jax.nn.sigmoid(g))).astype(a_ref.dtype)
    row = pl.program_id(0) * tm + lax.broadcasted_iota(jnp.int32, (tm, 1), 0)
    pos = jnp.where(row < n_lat, row % s, (row - n_lat) % c)
    seq_len = jnp.where(row < n_lat, s, c)
    local = lax.broadcasted_iota(jnp.int32, (tm, 1), 0)
    x = cc_ref[...].astype(_F32) * cu_ref[...].astype(_F32)
    x_halo_p = ccp_ref[7:8, :].astype(_F32) * cup_ref[7:8, :].astype(_F32)
    x_halo_n = ccn_ref[0:1, :].astype(_F32) * cun_ref[0:1, :].astype(_F32)
    x_prev = jnp.where(local == 0, x_halo_p, pltpu.roll(x, 1, axis=0))
    x_prev = jnp.where(pos == 0, 0.0, x_prev)
    x_next = jnp.where(local == tm - 1, x_halo_n, pltpu.roll(x, tm - 1, axis=0))
    x_next = jnp.where(pos == seq_len - 1, 0.0, x_next)
    y = cw_ref[0:1, :] * x_prev + cw_ref[1:2, :] * x + cw_ref[2:3, :] * x_next
    b_ref[...] = (cb_ref[...].astype(_F32) * y).astype(b_ref.dtype)


def _branches(o_fb, p, n_rows, hg_nw, conv_w, layer, b, s, c, mix_w, tm=256):
    tm = math.gcd(math.gcd(tm, s), b * c)
    last8 = p.shape[0] // 8 - 1
    main = lambda col: pl.BlockSpec((tm, mix_w), lambda i: (i, col))
    prev = lambda col: pl.BlockSpec((8, mix_w), lambda i: (jnp.maximum(i * (tm // 8) - 1, 0), col))
    nxt = lambda col: pl.BlockSpec((8, mix_w), lambda i: (jnp.minimum((i + 1) * (tm // 8), last8), col))
    out_sds = jax.ShapeDtypeStruct((n_rows, mix_w), _BF16)
    return pl.pallas_call(
        functools.partial(_branch_kernel, n_lat=b * s, s=s, c=c),
        grid=(n_rows // tm,),
        in_specs=[
            pl.BlockSpec((None, tm, mix_w), lambda i: (0, i, 0)),
            pl.BlockSpec((None, tm, mix_w), lambda i: (1, i, 0)),
            main(4), main(5), main(6), main(7), prev(6), prev(7), nxt(6), nxt(7),
            pl.BlockSpec((None, 1, HEAD_W), lambda i: (layer, 0, 0)),
            pl.BlockSpec((None, 3, mix_w), lambda i: (layer, 0, 0)),
        ],
        out_specs=[pl.BlockSpec((tm, mix_w), lambda i: (i, 0)), pl.BlockSpec((tm, mix_w), lambda i: (i, 0))],
        out_shape=[out_sds, out_sds],
        compiler_params=_params(("parallel",)),
        name="branch_prep",
    )(o_fb, o_fb, p, p, p, p, p, p, p, p, hg_nw.reshape(-1, 1, HEAD_W), conv_w)


def _merge_kernel(h_ref, mod_ref, a_ref, b_ref, c_ref, ga_ref, gb_ref, gc_ref, wa_ref, wb_ref, wc_ref,
                  wo_ref, o_ref):
    j = pl.program_id(1)

    @pl.when(j == 0)
    def _():
        o_ref[...] = jnp.zeros_like(o_ref)

    def gated(g_ref, x_ref, w_ref):
        return jax.nn.sigmoid(g_ref[...].astype(_F32)) * _dot(x_ref[...], w_ref[...])

    y = gated(ga_ref, a_ref, wa_ref) + gated(gb_ref, b_ref, wb_ref) + gated(gc_ref, c_ref, wc_ref)
    o_ref[...] += _dot(y.astype(_BF16), wo_ref[...])

    @pl.when(j == pl.num_programs(1) - 1)
    def _():
        o_ref[...] = h_ref[...] + mod_ref[5:6, :] * o_ref[...]


def _merge(h, n_rows, mod_l, a, bb, cc, p, w_branch, w_out, layer, dims, mix_w, tn=512):
    d = h.shape[1]
    tm = dims["tm"]
    mod_row = dims["mod_row"]
    g0 = 11 * mix_w // tn
    gstep = d // tn
    br = lambda: pl.BlockSpec((tm, mix_w), lambda i, j: (i, 0))
    gate = lambda k: pl.BlockSpec((tm, tn), lambda i, j: (i, g0 + k * gstep + j))
    wbr = lambda k: pl.BlockSpec((None, None, mix_w, tn), lambda i, j: (layer, k, 0, j))
    return pl.pallas_call(
        _merge_kernel,
        grid=(pl.cdiv(n_rows, tm), d // tn),
        in_specs=[
            pl.BlockSpec((tm, d), lambda i, j: (i, 0)),
            pl.BlockSpec((None, N_MOD, d), lambda i, j: (mod_row(i), 0, 0)),
            br(), br(), br(), gate(0), gate(1), gate(2), wbr(0), wbr(1), wbr(2),
            pl.BlockSpec((None, tn, d), lambda i, j: (layer, j, 0)),
        ],
        out_specs=pl.BlockSpec((tm, d), lambda i, j: (i, 0)),
        out_shape=jax.ShapeDtypeStruct((n_rows, d), _F32),
        compiler_params=_params(("parallel", "arbitrary")),
        name="merge_out",
    )(h, mod_l, a, bb, cc, p, p, p, w_branch, w_branch, w_branch, w_out)


def kernel(x, c, ctx, c_ctx, w_ada, b_ada, norm_w, ffn_w13, ffn_w2, w_in, hgrn_lb, hgrn_norm_w, conv_w,
           da_lambda, da_norm_w, w_branch, w_out, final_norm_w):
    b, s, d = x.shape
    cl = ctx.shape[1]
    depth = w_ada.shape[0]
    mix_w = d // 2
    n_lat, n_ctx = b * s, b * cl
    n_all = n_lat + n_ctx
    tm = math.gcd(math.gcd(512, s), n_ctx)
    dims = {
        "tm": tm,
        "tm_big": math.gcd(1024, s),
        "s": s,
        "b": b,
        "mod_row": lambda i: jnp.minimum(i * tm // s, b),
        "norm_w": norm_w.reshape(depth * 3, 1, d),
    }

    w13_b, w2_b, w_in_b = ffn_w13.astype(_BF16), ffn_w2.astype(_BF16), w_in.astype(_BF16)
    w_branch_b, w_out_b = w_branch.astype(_BF16), w_out.astype(_BF16)

    cvec = jnp.concatenate([c, c_ctx[None, :], jnp.zeros((8 - b - 1, d), _F32)], axis=0)
    mod = _ada_mod(cvec, w_ada, b_ada).reshape(depth, 8, N_MOD, d)

    cum = jnp.cumsum(jax.nn.softmax(hgrn_lb.astype(_F32), axis=1), axis=1)
    lb = cum - cum[:, :1]
    lb_tab = jnp.stack([jnp.log(lb), jnp.log1p(-lb), 1.0 - lb], axis=2)
    w_tab, lmap = _hgrn_tables()
    cos, sin = _rope_tables(s)

    h = jnp.concatenate([x.reshape(n_lat, d), ctx.reshape(n_ctx, d)], axis=0)
    for layer in range(depth):
        last = layer == depth - 1
        lam_init = 0.8 - 0.6 * math.exp(-0.3 * layer)
        mod_l = mod[layer]
        h = _ffn(h, n_all, mod_l, layer * 3, w13_b, w2_b, layer, 0, 0, dims)
        p = _mixer_in(h, mod_l, layer * 3 + 1, w_in_b, layer, dims)
        qk = _rope(p, cos, sin, b, s, 8 * mix_w)
        o_fb = _hgrn(p, lb_tab[:, layer], w_tab, lmap, b, s, cl, mix_w)
        tq = min(512, s)
        c_att = _attn(qk, 0, 0, tq, s // tq, p, qk, b, s, cl, mix_w, da_lambda, da_norm_w, layer, lam_init, True)
        n_rows = n_lat if last else n_all
        if not last:
            c_ctx_att = _attn(p, 8 * mix_w, n_lat, cl, 1, p, None, b, s, cl, mix_w, da_lambda, da_norm_w,
                              layer, lam_init, False)
            c_att = jnp.concatenate([c_att, c_ctx_att], axis=0)
        a_br, b_br = _branches(o_fb, p, n_rows, hgrn_norm_w, conv_w, layer, b, s, cl, mix_w)
        h = _merge(h, n_rows, mod_l, a_br, b_br, c_att, p, w_branch_b, w_out_b, layer, dims, mix_w)
        h = _ffn(h, n_rows, mod_l, layer * 3 + 2, w13_b, w2_b, layer, 1, 6, dims,
                 final_w=final_norm_w if last else None)
    return h.reshape(b, s, d)
```

```python
import functools
import math

import jax
import jax.numpy as jnp
import numpy as np
from jax import lax
from jax.experimental import pallas as pl
from jax.experimental.pallas import tpu as pltpu

_F32 = jnp.float32
_BF16 = jnp.bfloat16

EPS = 1e-6
N_MOD = 9
GRID_W = 64
ROPE_THETA = 10000.0
HEAD_W = 128
DA_HEAD_DIM = 64
LOG2_E = math.log2(math.e)
ATT_Q_SCALE = LOG2_E * DA_HEAD_DIM ** -0.5
ATT_Q_GROUP = 1024
ATT_ONES_ROWS = 16
HG_CHUNK = 64
HG_LEVELS = 6
ROW_CHUNK = 128
V7X_VMEM_LIMIT = 56 * 1024 * 1024
V7X_VMEM_LIMIT_FFN = 60 * 1024 * 1024


def _dot(a, b):
    return jnp.dot(a, b, preferred_element_type=_F32)


def _dot_nt(a, b):
    return lax.dot_general(a, b, (((1,), (1,)), ((), ())), preferred_element_type=_F32)


def _dot_tn(a, b):
    return lax.dot_general(a, b, (((0,), (0,)), ((), ())), preferred_element_type=_F32)


def _params(semantics, vmem=V7X_VMEM_LIMIT):
    return pltpu.CompilerParams(dimension_semantics=semantics, vmem_limit_bytes=vmem)


def _for_row_chunks(n_rows, body):
    def step(c, carry):
        body(pl.multiple_of(c * ROW_CHUNK, ROW_CHUNK))
        return carry
    lax.fori_loop(0, n_rows // ROW_CHUNK, step, 0)


def _norm_mod(h, nw, shift, scale):
    ms = jnp.mean(h * h, axis=-1, keepdims=True)
    return (h * lax.rsqrt(ms + EPS) * nw) * (1.0 + scale) + shift


def _ada_kernel(c_ref, w_ref, b_ref, o_ref):
    c = c_ref[...]
    s = (c * jax.nn.sigmoid(c)).astype(_BF16)
    o_ref[...] = _dot(s, w_ref[...].astype(_BF16)) + b_ref[...]


def _ada_mod(cvec, w_ada, b_ada, tn=1024):
    depth, d, nd = w_ada.shape
    rows = cvec.shape[0]
    return pl.pallas_call(
        _ada_kernel,
        grid=(depth, nd // tn),
        in_specs=[
            pl.BlockSpec((rows, d), lambda l, j: (0, 0)),
            pl.BlockSpec((None, d, tn), lambda l, j: (l, 0, j)),
            pl.BlockSpec((None, 1, tn), lambda l, j: (l, 0, j)),
        ],
        out_specs=pl.BlockSpec((None, rows, tn), lambda l, j: (l, 0, j)),
        out_shape=jax.ShapeDtypeStruct((depth, rows, nd), _F32),
        compiler_params=_params(("arbitrary", "arbitrary")),
        name="ada_mod",
    )(cvec, w_ada, b_ada.reshape(depth, 1, nd))


def _ffn_kernel(h_ref, mod_ref, nw_ref, w1_ref, w3_ref, w2_ref, *rest, mod_off, final_norm):
    if final_norm:
        fw_ref, o_ref, xn_ref = rest
    else:
        o_ref, xn_ref = rest
    j = pl.program_id(1)

    @pl.when(j == 0)
    def _():
        def rows(r0):
            rs = pl.ds(r0, ROW_CHUNK)
            xn = _norm_mod(h_ref[rs, :], nw_ref[...], mod_ref[mod_off:mod_off + 1, :],
                           mod_ref[mod_off + 1:mod_off + 2, :])
            xn_ref[rs, :] = xn.astype(_BF16)
            o_ref[rs, :] = jnp.zeros((ROW_CHUNK, o_ref.shape[1]), o_ref.dtype)
        _for_row_chunks(h_ref.shape[0], rows)

    xn = xn_ref[...]
    g = _dot(xn, w1_ref[...])
    u = _dot(xn, w3_ref[...])
    a = (g * jax.nn.sigmoid(g) * u).astype(_BF16)
    o_ref[...] += _dot(a, w2_ref[...])

    @pl.when(j == pl.num_programs(1) - 1)
    def _():
        gate = mod_ref[mod_off + 2:mod_off + 3, :]

        def rows(r0):
            rs = pl.ds(r0, ROW_CHUNK)
            out = h_ref[rs, :] + 0.5 * gate * o_ref[rs, :]
            if final_norm:
                ms = jnp.mean(out * out, axis=-1, keepdims=True)
                out = out * lax.rsqrt(ms + EPS) * fw_ref[...]
            o_ref[rs, :] = out
        _for_row_chunks(h_ref.shape[0], rows)


def _ffn(h, n_rows, mod_l, nw, w13, w2, layer, which, mod_off, dims, final_w=None, tf=512):
    d = h.shape[1]
    f = w2.shape[2]
    tm = dims["tm_big"]
    nfb = f // tf
    mod_row = lambda i: jnp.minimum(i * tm // dims["s"], dims["b"])
    in_specs = [
        pl.BlockSpec((tm, d), lambda i, j: (i, 0)),
        pl.BlockSpec((None, N_MOD, d), lambda i, j: (mod_row(i), 0, 0)),
        pl.BlockSpec((None, 1, d), lambda i, j: (nw, 0, 0)),
        pl.BlockSpec((None, None, d, tf), lambda i, j: (layer, which, 0, j)),
        pl.BlockSpec((None, None, d, tf), lambda i, j: (layer, which, 0, j + nfb)),
        pl.BlockSpec((None, None, tf, d), lambda i, j: (layer, which, j, 0)),
    ]
    args = [h, mod_l, dims["norm_w"], w13, w13, w2]
    if final_w is not None:
        in_specs.append(pl.BlockSpec((1, d), lambda i, j: (0, 0)))
        args.append(final_w.reshape(1, d))
    return pl.pallas_call(
        functools.partial(_ffn_kernel, mod_off=mod_off, final_norm=final_w is not None),
        grid=(pl.cdiv(n_rows, tm), nfb),
        in_specs=in_specs,
        out_specs=pl.BlockSpec((tm, d), lambda i, j: (i, 0)),
        out_shape=jax.ShapeDtypeStruct((n_rows, d), _F32),
        scratch_shapes=[pltpu.VMEM((tm, d), _BF16)],
        compiler_params=_params(("parallel", "arbitrary"), vmem=V7X_VMEM_LIMIT_FFN),
        name="half_ffn",
    )(*args)


def _proj_kernel(h_ref, mod_ref, nw_ref, w_ref, o_ref, xn_ref):
    @pl.when(pl.program_id(1) == 0)
    def _():
        xn = _norm_mod(h_ref[...], nw_ref[...], mod_ref[3:4, :], mod_ref[4:5, :])
        xn_ref[...] = xn.astype(_BF16)

    o_ref[...] = _dot(xn_ref[...], w_ref[...]).astype(o_ref.dtype)


def _mixer_in(h, mod_l, nw, w_in, layer, dims, tn=1024):
    n, d = h.shape
    cols = w_in.shape[2]
    tm = dims["tm_big"]
    mod_row = lambda i: jnp.minimum(i * tm // dims["s"], dims["b"])
    return pl.pallas_call(
        _proj_kernel,
        grid=(pl.cdiv(n, tm), cols // tn),
        in_specs=[
            pl.BlockSpec((tm, d), lambda i, j: (i, 0)),
            pl.BlockSpec((None, N_MOD, d), lambda i, j: (mod_row(i), 0, 0)),
            pl.BlockSpec((None, 1, d), lambda i, j: (nw, 0, 0)),
            pl.BlockSpec((None, d, tn), lambda i, j: (layer, 0, j)),
        ],
        out_specs=pl.BlockSpec((tm, tn), lambda i, j: (i, j)),
        out_shape=jax.ShapeDtypeStruct((n, cols), _BF16),
        scratch_shapes=[pltpu.VMEM((tm, d), _BF16)],
        compiler_params=_params(("parallel", "arbitrary")),
        name="mixer_in",
    )(h, mod_l, dims["norm_w"], w_in)


def _rope_kernel(x_ref, cos_ref, sin_ref, o_ref):
    cos = cos_ref[...]
    sin = sin_ref[...]
    lane = lax.broadcasted_iota(jnp.int32, cos.shape, 1)
    first = (lane % 32) < 16
    n_groups = x_ref.shape[1] // HEAD_W
    for g in range(n_groups):
        sl = slice(g * HEAD_W, (g + 1) * HEAD_W)
        x = x_ref[:, sl].astype(_F32)
        rot = jnp.where(first, pltpu.roll(x, HEAD_W - 16, axis=1), pltpu.roll(x, 16, axis=1))
        y = x * cos + rot * sin
        if g < n_groups // 2:
            y = y * ATT_Q_SCALE
        o_ref[:, sl] = y.astype(o_ref.dtype)


def _rope_tables(n_tokens):
    rows = n_tokens // GRID_W
    row = jnp.repeat(jnp.arange(rows, dtype=jnp.int32), GRID_W)
    col = jnp.tile(jnp.arange(GRID_W, dtype=jnp.int32), rows)
    half = DA_HEAD_DIM // 2
    inv_freq = ROPE_THETA ** (-jnp.arange(0, half, 2, dtype=_F32) / half)
    ang_r = row.astype(_F32)[:, None] * inv_freq
    ang_c = col.astype(_F32)[:, None] * inv_freq
    ang = jnp.concatenate([ang_r, ang_r, ang_c, ang_c], axis=-1)
    ang = jnp.concatenate([ang, ang], axis=-1)
    sign = jnp.where((jnp.arange(HEAD_W) % 32) < 16, -1.0, 1.0)
    return jnp.cos(ang), jnp.sin(ang) * sign


def _rope(p, cos, sin, b, s, col0, tr=512):
    width = 2 * dims_mix_w(p)
    tr = min(tr, s)
    return pl.pallas_call(
        _rope_kernel,
        grid=(b * s // tr,),
        in_specs=[
            pl.BlockSpec((tr, width), lambda i: (i, col0 // width)),
            pl.BlockSpec((tr, HEAD_W), lambda i: (i % (s // tr), 0)),
            pl.BlockSpec((tr, HEAD_W), lambda i: (i % (s // tr), 0)),
        ],
        out_specs=pl.BlockSpec((tr, width), lambda i: (i, 0)),
        out_shape=jax.ShapeDtypeStruct((b * s, width), _BF16),
        compiler_params=_params(("parallel",)),
        name="axial_rope",
    )(p, cos, sin)


def dims_mix_w(p):
    return p.shape[1] // 17


def _hgrn_tables():
    L = HG_CHUNK
    w = np.zeros((2, (HG_LEVELS + 2) * L + 8, L), np.float32)
    lmap = np.full((2, L, L), HG_LEVELS + 1, np.int32)
    r = np.arange(L)
    for t in range(L):
        w[0, t, r <= t] = 1
        w[0, L + t, r > t] = 1
        w[1, t, r >= t] = 1
        w[1, L + t, r < t] = 1
        lmap[:, t, t] = 0
        for lvl in range(1, HG_LEVELS + 1):
            bs = (2 * L) >> lvl
            mid = (t // bs) * bs + bs // 2
            base = (lvl + 1) * L + t
            if t >= mid:
                w[0, base, (r >= mid) & (r <= t)] = 1
                w[1, base, (r >= mid) & (r < t)] = 1
            else:
                w[0, base, (r > t) & (r < mid)] = 1
                w[1, base, (r >= t) & (r < mid)] = 1
            for s_ in range(L):
                if s_ // bs == t // bs and t >= mid and s_ < mid:
                    lmap[0, t, s_] = lvl
                    lmap[1, s_, t] = lvl
    w[:, (HG_LEVELS + 2) * L:, :] = 1
    w = np.concatenate([w, w], axis=2)
    return jnp.asarray(w, _BF16), jnp.asarray(lmap)


def _hgrn_kernel(q_ref, z_ref, v_ref, lb_ref, w_ref, lmap_ref, o_ref, st_ref):
    L = HG_CHUNK

    @pl.when(pl.program_id(2) == 0)
    def _():
        st_ref[...] = jnp.zeros_like(st_ref)

    heads = [slice(h * HEAD_W, (h + 1) * HEAD_W) for h in range(q_ref.shape[1] // HEAD_W)]
    lmap = lmap_ref[...]
    z = z_ref[...].astype(_F32)
    ls = jnp.minimum(z, 0.0) - jnp.log(1.0 + jnp.exp(-jnp.abs(z)))
    bb = lb_ref[1:2, :] + ls
    log_lb = lb_ref[0:1, :]
    lf = jnp.maximum(log_lb, bb) + jnp.log(1.0 + jnp.exp(-jnp.abs(log_lb - bb)))
    k = lb_ref[2:3, :] * jnp.exp(ls - z)
    lf2 = lf * LOG2_E
    lf_hi = lf2.astype(_BF16)
    lf_lo = (lf2 - lf_hi.astype(_F32)).astype(_BF16)
    e = jnp.exp2(_dot(w_ref[...], jnp.concatenate([lf_hi, lf_lo], axis=0)))
    q = q_ref[...].astype(_F32)
    v = v_ref[...]
    qk = q * k
    a = [jnp.where(lmap == 0, jnp.sum(qk[:, sl], axis=-1, keepdims=True) * jnp.ones((1, L), _F32), 0.0)
         for sl in heads]
    for lvl in range(1, HG_LEVELS + 1):
        el = e[(lvl + 1) * L:(lvl + 2) * L]
        ql = (q * el).astype(_BF16)
        kl = (k * el).astype(_BF16)
        a = [jnp.where(lmap == lvl, _dot_nt(ql[:, sl], kl[:, sl]), a[h]) for h, sl in enumerate(heads)]
    q_in = (q * e[0:L]).astype(_BF16)
    k_out = (k * e[L:2 * L]).astype(_BF16)
    tot = e[(HG_LEVELS + 2) * L:(HG_LEVELS + 2) * L + 1]
    for h, sl in enumerate(heads):
        st = st_ref[h]
        o_ref[:, sl] = _dot(a[h].astype(_BF16), v[:, sl]) + _dot_nt(q_in[:, sl], st.astype(_BF16))
        st_ref[h] = tot[:, sl] * st + _dot_tn(v[:, sl], k_out[:, sl])


def _hgrn(p, lb_tab, w_tab, lmap, b, s, c, mix_w):
    n = p.shape[0]
    L = HG_CHUNK
    ncc, ncl = c // L, s // L
    nheads = mix_w // HEAD_W

    def row_block(bi, di, j):
        is_ctx = j < ncc
        cj = jnp.where(di == 0, j, ncc - 1 - j)
        lj = jnp.where(di == 0, j - ncc, ncl - 1 - (j - ncc))
        return jnp.where(is_ctx, (b * s + bi * c) // L + cj, bi * ncl + lj)

    def spec(col):
        return pl.BlockSpec((L, mix_w), lambda bi, di, j: (row_block(bi, di, j), col))

    return pl.pallas_call(
        _hgrn_kernel,
        grid=(b, 2, ncc + ncl),
        in_specs=[
            spec(0),
            pl.BlockSpec((L, mix_w), lambda bi, di, j: (row_block(bi, di, j), 1 + di)),
            spec(3),
            pl.BlockSpec((None, 3, mix_w), lambda bi, di, j: (di, 0, 0)),
            pl.BlockSpec((None,) + w_tab.shape[1:], lambda bi, di, j: (di, 0, 0)),
            pl.BlockSpec((None, L, L), lambda bi, di, j: (di, 0, 0)),
        ],
        out_specs=pl.BlockSpec((None, L, mix_w), lambda bi, di, j: (di, row_block(bi, di, j), 0)),
        out_shape=jax.ShapeDtypeStruct((2, n, mix_w), _F32),
        scratch_shapes=[pltpu.VMEM((nheads, HEAD_W, HEAD_W), _F32)],
        compiler_params=_params(("arbitrary", "arbitrary", "arbitrary")),
        name="hgrn_scan",
    )(p, p, p, lb_tab, w_tab, lmap)


def _attn_kernel(*refs, has_lat, tk, lam_init, q_scale):
    if has_lat:
        (q_ref, kc_ref, vc_ref, kl_ref, vl_ref, lamp_ref, nw_ref, o_ref, m_ref, acc_ref, vtc_ref, vtl_ref,
         st_ref, mx_ref) = refs
    else:
        q_ref, kc_ref, vc_ref, lamp_ref, nw_ref, o_ref, m_ref, acc_ref, vtc_ref = refs
    tq = q_ref.shape[0]
    eye = (lax.broadcasted_iota(jnp.int32, (HEAD_W, HEAD_W), 0)
           == lax.broadcasted_iota(jnp.int32, (HEAD_W, HEAD_W), 1)).astype(_F32).astype(_BF16)

    @pl.when(pl.program_id(2) == 0)
    def _():
        ones = jnp.ones((ATT_ONES_ROWS, vtc_ref.shape[1]), _BF16)
        vtc_ref[0:HEAD_W, :] = _dot_nt(eye, vc_ref[...]).astype(_BF16)
        vtc_ref[HEAD_W:, :] = ones
        if has_lat:
            def tbody(i, carry):
                off = pl.multiple_of(i * tk, tk)
                vtl_ref[0:HEAD_W, pl.ds(off, tk)] = _dot_nt(eye, vl_ref[pl.ds(off, tk), :]).astype(_BF16)
                vtl_ref[HEAD_W:, pl.ds(off, tk)] = jnp.ones((ATT_ONES_ROWS, tk), _BF16)
                return carry
            lax.fori_loop(0, vl_ref.shape[0] // tk, tbody, 0)

    q = q_ref[...].astype(_F32)
    if q_scale is not None:
        q = q * q_scale
    lane = lax.broadcasted_iota(jnp.int32, q.shape, 1)
    qq = jnp.concatenate([jnp.where(lane < DA_HEAD_DIM, q, 0.0), jnp.where(lane >= DA_HEAD_DIM, q, 0.0)],
                         axis=0).astype(_BF16)

    m_ref[...] = jnp.full_like(m_ref, -jnp.inf)
    acc_ref[...] = jnp.zeros_like(acc_ref)

    def softmax_pv(st, st_max, vt):
        m_prev = m_ref[...]
        m_new = jnp.maximum(m_prev, st_max)
        alpha = jnp.exp2(m_prev - m_new)
        pt = jnp.exp2(st - m_new).astype(_BF16)
        acc_ref[...] = alpha * acc_ref[...] + _dot(vt, pt)
        m_ref[...] = m_new

    def scores(t, buf):
        off = pl.multiple_of((t - 1) * tk, tk)
        st = _dot_nt(kl_ref[pl.ds(off, tk), :], qq)
        st_ref[buf] = st
        mx_ref[buf] = jnp.max(st, axis=0, keepdims=True)

    st_c = _dot_nt(kc_ref[...], qq)
    mx_c = jnp.max(st_c, axis=0, keepdims=True)
    if not has_lat:
        softmax_pv(st_c, mx_c, vtc_ref[...])
    else:
        n = kl_ref.shape[0] // tk

        def consume(t, buf):
            off = pl.multiple_of((t - 1) * tk, tk)
            softmax_pv(st_ref[buf], mx_ref[buf], vtl_ref[:, pl.ds(off, tk)])

        scores(1, 1)
        softmax_pv(st_c, mx_c, vtc_ref[...])
        n_pairs = (n - 1) // 2

        def body(j, carry):
            t = 2 * j + 1
            scores(t + 1, 0)
            consume(t, 1)
            scores(t + 2, 1)
            consume(t + 1, 0)
            return carry
        lax.fori_loop(0, n_pairs, body, 0)
        for t in range(2 * n_pairs + 1, n + 1):
            if t < n:
                scores(t + 1, (t + 1) % 2)
            consume(t, t % 2)

    lp = lamp_ref[...]
    lam = (jnp.exp(jnp.sum(lp[0:1] * lp[1:2], axis=-1, keepdims=True))
           - jnp.exp(jnp.sum(lp[2:3] * lp[3:4], axis=-1, keepdims=True)) + lam_init)
    on = acc_ref[0:HEAD_W, :] / acc_ref[HEAD_W:HEAD_W + 1, :]
    ot = on[:, 0:tq] - lam * on[:, tq:2 * tq]
    ms = jnp.mean(ot * ot, axis=0, keepdims=True)
    o = (ot * lax.rsqrt(ms + EPS)).T
    o_ref[...] = (o * nw_ref[...] * (1.0 - lam_init)).astype(o_ref.dtype)


def _attn(q_arr, q_col0, q_row0, tq, nq, p, kl_arr, b, s, c, mix_w, lam_p, nw, layer, lam_init, has_lat, tk=1024):
    nheads = mix_w // HEAD_W
    ak0, av0 = 9 * mix_w // HEAD_W, 10 * mix_w // HEAD_W
    ctx_blk0 = b * s // c
    in_specs = [
        pl.BlockSpec((tq, HEAD_W), lambda bi, h, qi: (q_row0 // tq + bi * nq + qi, q_col0 // HEAD_W + h)),
        pl.BlockSpec((c, HEAD_W), lambda bi, h, qi: (ctx_blk0 + bi, ak0 + h)),
        pl.BlockSpec((c, HEAD_W), lambda bi, h, qi: (ctx_blk0 + bi, av0 + h)),
    ]
    args = [q_arr, p, p]
    if has_lat:
        in_specs += [
            pl.BlockSpec((s, HEAD_W), lambda bi, h, qi: (bi, nheads + h)),
            pl.BlockSpec((s, HEAD_W), lambda bi, h, qi: (bi, av0 + h)),
        ]
        args += [kl_arr, p]
    in_specs += [
        pl.BlockSpec((None, 4, DA_HEAD_DIM), lambda bi, h, qi: (layer, 0, 0)),
        pl.BlockSpec((None, 1, HEAD_W), lambda bi, h, qi: (layer, 0, 0)),
    ]
    args += [lam_p, nw.reshape(-1, 1, HEAD_W)]
    vt_rows = HEAD_W + ATT_ONES_ROWS
    scratch = [pltpu.VMEM((1, 2 * tq), _F32), pltpu.VMEM((vt_rows, 2 * tq), _F32), pltpu.VMEM((vt_rows, c), _BF16)]
    tk = min(tk, s)
    if has_lat:
        scratch += [pltpu.VMEM((vt_rows, s), _BF16), pltpu.VMEM((2, tk, 2 * tq), _F32),
                    pltpu.VMEM((2, 1, 2 * tq), _F32)]
    return pl.pallas_call(
        functools.partial(_attn_kernel, has_lat=has_lat, tk=tk, lam_init=lam_init,
                          q_scale=None if has_lat else ATT_Q_SCALE),
        grid=(b, nheads, nq),
        in_specs=in_specs,
        out_specs=pl.BlockSpec((tq, HEAD_W), lambda bi, h, qi: (bi * nq + qi, h)),
        out_shape=jax.ShapeDtypeStruct((b * nq * tq, mix_w), _BF16),
        scratch_shapes=scratch,
        compiler_params=_params(("arbitrary", "arbitrary", "arbitrary")),
        name="diff_attn_lat" if has_lat else "diff_attn_ctx",
    )(*args)


def _branch_kernel(of_ref, ob_ref, hg_ref, cb_ref, cc_ref, cu_ref, ccp_ref, cup_ref, ccn_ref, cun_ref,
                   hnw_ref, cw_ref, a_ref, b_ref, *, n_lat, s, c):
    tm = of_ref.shape[0]
    nw = hnw_ref[...]
    for h in range(of_ref.shape[1] // HEAD_W):
        sl = slice(h * HEAD_W, (h + 1) * HEAD_W)
        o = of_ref[:, sl] + ob_ref[:, sl]
        ms = jnp.mean(o * o, axis=-1, keepdims=True)
        g = hg_ref[:, sl].astype(_F32)
        a_ref[:, sl] = ((o * lax.rsqrt(ms + EPS) * nw) * (g * jax.nn.sigmoid(g))).astype(a_ref.dtype)
    row = pl.program_id(0) * tm + lax.broadcasted_iota(jnp.int32, (tm, 1), 0)
    pos = jnp.where(row < n_lat, row % s, (row - n_lat) % c)
    seq_len = jnp.where(row < n_lat, s, c)
    local = lax.broadcasted_iota(jnp.int32, (tm, 1), 0)
    x = cc_ref[...].astype(_F32) * cu_ref[...].astype(_F32)
    x_halo_p = ccp_ref[7:8, :].astype(_F32) * cup_ref[7:8, :].astype(_F32)
    x_halo_n = ccn_ref[0:1, :].astype(_F32) * cun_ref[0:1, :].astype(_F32)
    x_prev = jnp.where(local == 0, x_halo_p, pltpu.roll(x, 1, axis=0))
    x_prev = jnp.where(pos == 0, 0.0, x_prev)
    x_next = jnp.where(local == tm - 1, x_halo_n, pltpu.roll(x, tm - 1, axis=0))
    x_next = jnp.where(pos == seq_len - 1, 0.0, x_next)
    y = cw_ref[0:1, :] * x_prev + cw_ref[1:2, :] * x + cw_ref[2:3, :] * x_next
    b_ref[...] = (cb_ref[...].astype(_F32) * y).astype(b_ref.dtype)


def _branches(o_fb, p, n_rows, hg_nw, conv_w, layer, b, s, c, mix_w, tm=256):
    tm = math.gcd(math.gcd(tm, s), b * c)
    last8 = p.shape[0] // 8 - 1
    main = lambda col: pl.BlockSpec((tm, mix_w), lambda i: (i, col))
    prev = lambda col: pl.BlockSpec((8, mix_w), lambda i: (jnp.maximum(i * (tm // 8) - 1, 0), col))
    nxt = lambda col: pl.BlockSpec((8, mix_w), lambda i: (jnp.minimum((i + 1) * (tm // 8), last8), col))
    out_sds = jax.ShapeDtypeStruct((n_rows, mix_w), _BF16)
    return pl.pallas_call(
        functools.partial(_branch_kernel, n_lat=b * s, s=s, c=c),
        grid=(n_rows // tm,),
        in_specs=[
            pl.BlockSpec((None, tm, mix_w), lambda i: (0, i, 0)),
            pl.BlockSpec((None, tm, mix_w), lambda i: (1, i, 0)),
            main(4), main(5), main(6), main(7), prev(6), prev(7), nxt(6), nxt(7),
            pl.BlockSpec((None, 1, HEAD_W), lambda i: (layer, 0, 0)),
            pl.BlockSpec((None, 3, mix_w), lambda i: (layer, 0, 0)),
        ],
        out_specs=[pl.BlockSpec((tm, mix_w), lambda i: (i, 0)), pl.BlockSpec((tm, mix_w), lambda i: (i, 0))],
        out_shape=[out_sds, out_sds],
        compiler_params=_params(("parallel",)),
        name="branch_prep",
    )(o_fb, o_fb, p, p, p, p, p, p, p, p, hg_nw.reshape(-1, 1, HEAD_W), conv_w)


def _merge_kernel(h_ref, mod_ref, a_ref, b_ref, c_ref, ga_ref, gb_ref, gc_ref, wa_ref, wb_ref, wc_ref,
                  wo_ref, o_ref):
    j = pl.program_id(1)

    @pl.when(j == 0)
    def _():
        o_ref[...] = jnp.zeros_like(o_ref)

    def gated(g_ref, x_ref, w_ref):
        return jax.nn.sigmoid(g_ref[...].astype(_F32)) * _dot(x_ref[...], w_ref[...])

    y = gated(ga_ref, a_ref, wa_ref) + gated(gb_ref, b_ref, wb_ref) + gated(gc_ref, c_ref, wc_ref)
    o_ref[...] += _dot(y.astype(_BF16), wo_ref[...])

    @pl.when(j == pl.num_programs(1) - 1)
    def _():
        o_ref[...] = h_ref[...] + mod_ref[5:6, :] * o_ref[...]


def _merge(h, n_rows, mod_l, a, bb, cc, p, w_branch, w_out, layer, dims, mix_w, tn=512):
    d = h.shape[1]
    tm = dims["tm"]
    mod_row = dims["mod_row"]
    g0 = 11 * mix_w // tn
    gstep = d // tn
    br = lambda: pl.BlockSpec((tm, mix_w), lambda i, j: (i, 0))
    gate = lambda k: pl.BlockSpec((tm, tn), lambda i, j: (i, g0 + k * gstep + j))
    wbr = lambda k: pl.BlockSpec((None, None, mix_w, tn), lambda i, j: (layer, k, 0, j))
    return pl.pallas_call(
        _merge_kernel,
        grid=(pl.cdiv(n_rows, tm), d // tn),
        in_specs=[
            pl.BlockSpec((tm, d), lambda i, j: (i, 0)),
            pl.BlockSpec((None, N_MOD, d), lambda i, j: (mod_row(i), 0, 0)),
            br(), br(), br(), gate(0), gate(1), gate(2), wbr(0), wbr(1), wbr(2),
            pl.BlockSpec((None, tn, d), lambda i, j: (layer, j, 0)),
        ],
        out_specs=pl.BlockSpec((tm, d), lambda i, j: (i, 0)),
        out_shape=jax.ShapeDtypeStruct((n_rows, d), _F32),
        compiler_params=_params(("parallel", "arbitrary")),
        name="merge_out",
    )(h, mod_l, a, bb, cc, p, p, p, w_branch, w_branch, w_branch, w_out)


def kernel(x, c, ctx, c_ctx, w_ada, b_ada, norm_w, ffn_w13, ffn_w2, w_in, hgrn_lb, hgrn_norm_w, conv_w,
           da_lambda, da_norm_w, w_branch, w_out, final_norm_w):
    b, s, d = x.shape
    cl = ctx.shape[1]
    depth = w_ada.shape[0]
    mix_w = d // 2
    n_lat, n_ctx = b * s, b * cl
    n_all = n_lat + n_ctx
    tm = math.gcd(math.gcd(512, s), n_ctx)
    dims = {
        "tm": tm,
        "tm_big": math.gcd(1024, s),
        "s": s,
        "b": b,
        "mod_row": lambda i: jnp.minimum(i * tm // s, b),
        "norm_w": norm_w.reshape(depth * 3, 1, d),
    }

    w13_b, w2_b, w_in_b = ffn_w13.astype(_BF16), ffn_w2.astype(_BF16), w_in.astype(_BF16)
    w_branch_b, w_out_b = w_branch.astype(_BF16), w_out.astype(_BF16)

    cvec = jnp.concatenate([c, c_ctx[None, :], jnp.zeros((8 - b - 1, d), _F32)], axis=0)
    mod = _ada_mod(cvec, w_ada, b_ada).reshape(depth, 8, N_MOD, d)

    cum = jnp.cumsum(jax.nn.softmax(hgrn_lb.astype(_F32), axis=1), axis=1)
    lb = cum - cum[:, :1]
    lb_tab = jnp.stack([jnp.log(lb), jnp.log1p(-lb), 1.0 - lb], axis=2)
    w_tab, lmap = _hgrn_tables()
    cos, sin = _rope_tables(s)

    h = jnp.concatenate([x.reshape(n_lat, d), ctx.reshape(n_ctx, d)], axis=0)
    for layer in range(depth):
        last = layer == depth - 1
        lam_init = 0.8 - 0.6 * math.exp(-0.3 * layer)
        mod_l = mod[layer]
        h = _ffn(h, n_all, mod_l, layer * 3, w13_b, w2_b, layer, 0, 0, dims)
        p = _mixer_in(h, mod_l, layer * 3 + 1, w_in_b, layer, dims)
        qk = _rope(p, cos, sin, b, s, 8 * mix_w)
        o_fb = _hgrn(p, lb_tab[:, layer], w_tab, lmap, b, s, cl, mix_w)
        tq = min(512, s)
        c_att = _attn(qk, 0, 0, tq, s // tq, p, qk, b, s, cl, mix_w, da_lambda, da_norm_w, layer, lam_init, True)
        n_rows = n_lat if last else n_all
        if not last:
            c_ctx_att = _attn(p, 8 * mix_w, n_lat, cl, 1, p, None, b, s, cl, mix_w, da_lambda, da_norm_w,
                              layer, lam_init, False)
            c_att = jnp.concatenate([c_att, c_ctx_att], axis=0)
        a_br, b_br = _branches(o_fb, p, n_rows, hgrn_norm_w, conv_w, layer, b, s, cl, mix_w)
        h = _merge(h, n_rows, mod_l, a_br, b_br, c_att, p, w_branch_b, w_out_b, layer, dims, mix_w)
        h = _ffn(h, n_rows, mod_l, layer * 3 + 2, w13_b, w2_b, layer, 1, 6, dims,
                 final_w=final_norm_w if last else None)
    return h.reshape(b, s, d)
```

```python
import functools
import math

import jax
import jax.numpy as jnp
import numpy as np
from jax import lax
from jax.experimental import pallas as pl
from jax.experimental.pallas import tpu as pltpu

_F32 = jnp.float32
_BF16 = jnp.bfloat16

EPS = 1e-6
N_MOD = 9
GRID_W = 64
ROPE_THETA = 10000.0
HEAD_W = 128
DA_HEAD_DIM = 64
LOG2_E = math.log2(math.e)
ATT_Q_SCALE = LOG2_E * DA_HEAD_DIM ** -0.5
ATT_Q_GROUP = 1024
ATT_ONES_ROWS = 16
HG_CHUNK = 64
HG_LEVELS = 6
HG_HEAD_GROUP = 8
ROW_CHUNK = 128
V7X_VMEM_LIMIT = 56 * 1024 * 1024
V7X_VMEM_LIMIT_FFN = 60 * 1024 * 1024


def _dot(a, b):
    return jnp.dot(a, b, preferred_element_type=_F32)


def _dot_nt(a, b):
    return lax.dot_general(a, b, (((1,), (1,)), ((), ())), preferred_element_type=_F32)


def _dot_tn(a, b):
    return lax.dot_general(a, b, (((0,), (0,)), ((), ())), preferred_element_type=_F32)


def _params(semantics, vmem=V7X_VMEM_LIMIT):
    return pltpu.CompilerParams(dimension_semantics=semantics, vmem_limit_bytes=vmem)


def _for_row_chunks(n_rows, body):
    def step(c, carry):
        body(pl.multiple_of(c * ROW_CHUNK, ROW_CHUNK))
        return carry
    lax.fori_loop(0, n_rows // ROW_CHUNK, step, 0)


def _norm_mod(h, nw, shift, scale):
    ms = jnp.mean(h * h, axis=-1, keepdims=True)
    return (h * lax.rsqrt(ms + EPS) * nw) * (1.0 + scale) + shift


def _ada_kernel(c_ref, w_ref, b_ref, o_ref):
    c = c_ref[...]
    s = (c * jax.nn.sigmoid(c)).astype(_BF16)
    o_ref[...] = _dot(s, w_ref[...].astype(_BF16)) + b_ref[...]


def _ada_mod(cvec, w_ada, b_ada, tn=1024):
    depth, d, nd = w_ada.shape
    rows = cvec.shape[0]
    return pl.pallas_call(
        _ada_kernel,
        grid=(depth, nd // tn),
        in_specs=[
            pl.BlockSpec((rows, d), lambda l, j: (0, 0)),
            pl.BlockSpec((None, d, tn), lambda l, j: (l, 0, j)),
            pl.BlockSpec((None, 1, tn), lambda l, j: (l, 0, j)),
        ],
        out_specs=pl.BlockSpec((None, rows, tn), lambda l, j: (l, 0, j)),
        out_shape=jax.ShapeDtypeStruct((depth, rows, nd), _F32),
        compiler_params=_params(("arbitrary", "arbitrary")),
        name="ada_mod",
    )(cvec, w_ada, b_ada.reshape(depth, 1, nd))


def _ffn_kernel(h_ref, mod_ref, nw_ref, w1_ref, w3_ref, w2_ref, *rest, mod_off, final_norm):
    if final_norm:
        fw_ref, o_ref, xn_ref = rest
    else:
        o_ref, xn_ref = rest
    j = pl.program_id(1)

    @pl.when(j == 0)
    def _():
        def rows(r0):
            rs = pl.ds(r0, ROW_CHUNK)
            xn = _norm_mod(h_ref[rs, :], nw_ref[...], mod_ref[mod_off:mod_off + 1, :],
                           mod_ref[mod_off + 1:mod_off + 2, :])
            xn_ref[rs, :] = xn.astype(_BF16)
            o_ref[rs, :] = jnp.zeros((ROW_CHUNK, o_ref.shape[1]), o_ref.dtype)
        _for_row_chunks(h_ref.shape[0], rows)

    xn = xn_ref[...]
    g = _dot(xn, w1_ref[...])
    u = _dot(xn, w3_ref[...])
    a = (g * jax.nn.sigmoid(g) * u).astype(_BF16)
    o_ref[...] += _dot(a, w2_ref[...])

    @pl.when(j == pl.num_programs(1) - 1)
    def _():
        gate = mod_ref[mod_off + 2:mod_off + 3, :]

        def rows(r0):
            rs = pl.ds(r0, ROW_CHUNK)
            out = h_ref[rs, :] + 0.5 * gate * o_ref[rs, :]
            if final_norm:
                ms = jnp.mean(out * out, axis=-1, keepdims=True)
                out = out * lax.rsqrt(ms + EPS) * fw_ref[...]
            o_ref[rs, :] = out
        _for_row_chunks(h_ref.shape[0], rows)


def _ffn(h, n_rows, mod_l, nw, w13, w2, layer, which, mod_off, dims, final_w=None, tf=512):
    d = h.shape[1]
    f = w2.shape[2]
    tm = dims["tm_big"]
    nfb = f // tf
    mod_row = lambda i: jnp.minimum(i * tm // dims["s"], dims["b"])
    in_specs = [
        pl.BlockSpec((tm, d), lambda i, j: (i, 0)),
        pl.BlockSpec((None, N_MOD, d), lambda i, j: (mod_row(i), 0, 0)),
        pl.BlockSpec((None, 1, d), lambda i, j: (nw, 0, 0)),
        pl.BlockSpec((None, None, d, tf), lambda i, j: (layer, which, 0, j)),
        pl.BlockSpec((None, None, d, tf), lambda i, j: (layer, which, 0, j + nfb)),
        pl.BlockSpec((None, None, tf, d), lambda i, j: (layer, which, j, 0)),
    ]
    args = [h, mod_l, dims["norm_w"], w13, w13, w2]
    if final_w is not None:
        in_specs.append(pl.BlockSpec((1, d), lambda i, j: (0, 0)))
        args.append(final_w.reshape(1, d))
    return pl.pallas_call(
        functools.partial(_ffn_kernel, mod_off=mod_off, final_norm=final_w is not None),
        grid=(pl.cdiv(n_rows, tm), nfb),
        in_specs=in_specs,
        out_specs=pl.BlockSpec((tm, d), lambda i, j: (i, 0)),
        out_shape=jax.ShapeDtypeStruct((n_rows, d), _F32),
        scratch_shapes=[pltpu.VMEM((tm, d), _BF16)],
        compiler_params=_params(("parallel", "arbitrary"), vmem=V7X_VMEM_LIMIT_FFN),
        name="half_ffn",
    )(*args)


def _proj_kernel(h_ref, mod_ref, nw_ref, w_ref, o_ref, xn_ref):
    @pl.when(pl.program_id(1) == 0)
    def _():
        xn = _norm_mod(h_ref[...], nw_ref[...], mod_ref[3:4, :], mod_ref[4:5, :])
        xn_ref[...] = xn.astype(_BF16)

    o_ref[...] = _dot(xn_ref[...], w_ref[...]).astype(o_ref.dtype)


def _mixer_in(h, mod_l, nw, w_in, layer, dims, tn=1024):
    n, d = h.shape
    cols = w_in.shape[2]
    tm = dims["tm_big"]
    mod_row = lambda i: jnp.minimum(i * tm // dims["s"], dims["b"])
    return pl.pallas_call(
        _proj_kernel,
        grid=(pl.cdiv(n, tm), cols // tn),
        in_specs=[
            pl.BlockSpec((tm, d), lambda i, j: (i, 0)),
            pl.BlockSpec((None, N_MOD, d), lambda i, j: (mod_row(i), 0, 0)),
            pl.BlockSpec((None, 1, d), lambda i, j: (nw, 0, 0)),
            pl.BlockSpec((None, d, tn), lambda i, j: (layer, 0, j)),
        ],
        out_specs=pl.BlockSpec((tm, tn), lambda i, j: (i, j)),
        out_shape=jax.ShapeDtypeStruct((n, cols), _BF16),
        scratch_shapes=[pltpu.VMEM((tm, d), _BF16)],
        compiler_params=_params(("parallel", "arbitrary")),
        name="mixer_in",
    )(h, mod_l, dims["norm_w"], w_in)


def _rope_kernel(x_ref, v_ref, cos_ref, sin_ref, o_ref):
    cos = cos_ref[...]
    sin = sin_ref[...]
    lane = lax.broadcasted_iota(jnp.int32, cos.shape, 1)
    first = (lane % 32) < 16
    n_groups = x_ref.shape[1] // HEAD_W
    for g in range(n_groups):
        sl = slice(g * HEAD_W, (g + 1) * HEAD_W)
        x = x_ref[:, sl].astype(_F32)
        rot = jnp.where(first, pltpu.roll(x, HEAD_W - 16, axis=1), pltpu.roll(x, 16, axis=1))
        y = x * cos + rot * sin
        if g < n_groups // 2:
            y = y * ATT_Q_SCALE
        o_ref[g] = y.astype(o_ref.dtype)
    for g in range(v_ref.shape[1] // HEAD_W):
        o_ref[n_groups + g] = v_ref[:, g * HEAD_W:(g + 1) * HEAD_W]


def _rope_tables(n_tokens):
    rows = n_tokens // GRID_W
    row = jnp.repeat(jnp.arange(rows, dtype=jnp.int32), GRID_W)
    col = jnp.tile(jnp.arange(GRID_W, dtype=jnp.int32), rows)
    half = DA_HEAD_DIM // 2
    inv_freq = ROPE_THETA ** (-jnp.arange(0, half, 2, dtype=_F32) / half)
    ang_r = row.astype(_F32)[:, None] * inv_freq
    ang_c = col.astype(_F32)[:, None] * inv_freq
    ang = jnp.concatenate([ang_r, ang_r, ang_c, ang_c], axis=-1)
    ang = jnp.concatenate([ang, ang], axis=-1)
    sign = jnp.where((jnp.arange(HEAD_W) % 32) < 16, -1.0, 1.0)
    return jnp.cos(ang), jnp.sin(ang) * sign


def _rope(p, cos, sin, b, s, col0, tr=512):
    mix_w = dims_mix_w(p)
    width = 2 * mix_w
    tr = min(tr, s)
    return pl.pallas_call(
        _rope_kernel,
        grid=(b * s // tr,),
        in_specs=[
            pl.BlockSpec((tr, width), lambda i: (i, col0 // width)),
            pl.BlockSpec((tr, mix_w), lambda i: (i, col0 // mix_w + 2)),
            pl.BlockSpec((tr, HEAD_W), lambda i: (i % (s // tr), 0)),
            pl.BlockSpec((tr, HEAD_W), lambda i: (i % (s // tr), 0)),
        ],
        out_specs=pl.BlockSpec((3 * mix_w // HEAD_W, tr, HEAD_W), lambda i: (0, i, 0)),
        out_shape=jax.ShapeDtypeStruct((3 * mix_w // HEAD_W, b * s, HEAD_W), _BF16),
        compiler_params=_params(("parallel",)),
        name="axial_rope",
    )(p, p, cos, sin)


def dims_mix_w(p):
    return p.shape[1] // 17


def _hgrn_tables():
    L = HG_CHUNK
    w = np.zeros((2, (HG_LEVELS + 2) * L + 8, L), np.float32)
    lmap = np.full((2, L, L), HG_LEVELS + 1, np.int32)
    r = np.arange(L)
    for t in range(L):
        w[0, t, r <= t] = 1
        w[0, L + t, r > t] = 1
        w[1, t, r >= t] = 1
        w[1, L + t, r < t] = 1
        lmap[:, t, t] = 0
        for lvl in range(1, HG_LEVELS + 1):
            bs = (2 * L) >> lvl
            mid = (t // bs) * bs + bs // 2
            base = (lvl + 1) * L + t
            if t >= mid:
                w[0, base, (r >= mid) & (r <= t)] = 1
                w[1, base, (r >= mid) & (r < t)] = 1
            else:
                w[0, base, (r > t) & (r < mid)] = 1
                w[1, base, (r >= t) & (r < mid)] = 1
            for s_ in range(L):
                if s_ // bs == t // bs and t >= mid and s_ < mid:
                    lmap[0, t, s_] = lvl
                    lmap[1, s_, t] = lvl
    w[:, (HG_LEVELS + 2) * L:, :] = 1
    w = np.concatenate([w, w], axis=2)
    return jnp.asarray(w, _BF16), jnp.asarray(lmap)


def _hgrn_kernel(q_ref, z_ref, v_ref, lb_ref, w_ref, lmap_ref, o_ref, st_ref):
    L = HG_CHUNK

    @pl.when(pl.program_id(2) == 0)
    def _():
        st_ref[...] = jnp.zeros_like(st_ref)

    n_heads = q_ref.shape[1] // HEAD_W
    lmap = lmap_ref[...]
    for h0 in range(0, n_heads, HG_HEAD_GROUP):
        gs = slice(h0 * HEAD_W, (h0 + HG_HEAD_GROUP) * HEAD_W)
        heads = [slice(h * HEAD_W, (h + 1) * HEAD_W) for h in range(HG_HEAD_GROUP)]
        z = z_ref[:, gs].astype(_F32)
        ls = jnp.minimum(z, 0.0) - jnp.log(1.0 + jnp.exp(-jnp.abs(z)))
        bb = lb_ref[1:2, gs] + ls
        log_lb = lb_ref[0:1, gs]
        lf = jnp.maximum(log_lb, bb) + jnp.log(1.0 + jnp.exp(-jnp.abs(log_lb - bb)))
        k = lb_ref[2:3, gs] * jnp.exp(ls - z)
        lf2 = lf * LOG2_E
        lf_hi = lf2.astype(_BF16)
        lf_lo = (lf2 - lf_hi.astype(_F32)).astype(_BF16)
        e = jnp.exp2(_dot(w_ref[...], jnp.concatenate([lf_hi, lf_lo], axis=0)))
        q = q_ref[:, gs].astype(_F32)
        v = v_ref[:, gs]
        qk = q * k
        a = [jnp.where(lmap == 0, jnp.sum(qk[:, sl], axis=-1, keepdims=True) * jnp.ones((1, L), _F32), 0.0)
             for sl in heads]
        for lvl in range(1, HG_LEVELS + 1):
            el = e[(lvl + 1) * L:(lvl + 2) * L]
            ql = (q * el).astype(_BF16)
            kl = (k * el).astype(_BF16)
            a = [jnp.where(lmap == lvl, _dot_nt(ql[:, sl], kl[:, sl]), a[h]) for h, sl in enumerate(heads)]
        q_in = (q * e[0:L]).astype(_BF16)
        k_out = (k * e[L:2 * L]).astype(_BF16)
        tot = e[(HG_LEVELS + 2) * L:(HG_LEVELS + 2) * L + 1]
        for h, sl in enumerate(heads):
            st = st_ref[h0 + h]
            osl = slice((h0 + h) * HEAD_W, (h0 + h + 1) * HEAD_W)
            o_ref[:, osl] = _dot(a[h].astype(_BF16), v[:, sl]) + _dot_nt(q_in[:, sl], st.astype(_BF16))
            st_ref[h0 + h] = tot[:, sl] * st + _dot_tn(v[:, sl], k_out[:, sl])


def _hgrn(p, lb_tab, w_tab, lmap, b, s, c, mix_w):
    n = p.shape[0]
    L = HG_CHUNK
    ncc, ncl = c // L, s // L
    nheads = mix_w // HEAD_W

    def row_block(bi, di, j):
        is_ctx = j < ncc
        cj = jnp.where(di == 0, j, ncc - 1 - j)
        lj = jnp.where(di == 0, j - ncc, ncl - 1 - (j - ncc))
        return jnp.where(is_ctx, (b * s + bi * c) // L + cj, bi * ncl + lj)

    def spec(col):
        return pl.BlockSpec((L, mix_w), lambda bi, di, j: (row_block(bi, di, j), col))

    return pl.pallas_call(
        _hgrn_kernel,
        grid=(b, 2, ncc + ncl),
        in_specs=[
            spec(0),
            pl.BlockSpec((L, mix_w), lambda bi, di, j: (row_block(bi, di, j), 1 + di)),
            spec(3),
            pl.BlockSpec((None, 3, mix_w), lambda bi, di, j: (di, 0, 0)),
            pl.BlockSpec((None,) + w_tab.shape[1:], lambda bi, di, j: (di, 0, 0)),
            pl.BlockSpec((None, L, L), lambda bi, di, j: (di, 0, 0)),
        ],
        out_specs=pl.BlockSpec((None, L, mix_w), lambda bi, di, j: (di, row_block(bi, di, j), 0)),
        out_shape=jax.ShapeDtypeStruct((2, n, mix_w), _F32),
        scratch_shapes=[pltpu.VMEM((nheads, HEAD_W, HEAD_W), _F32)],
        compiler_params=_params(("arbitrary", "arbitrary", "arbitrary")),
        name="hgrn_scan",
    )(p, p, p, lb_tab, w_tab, lmap)


def _attn_kernel(*refs, has_lat, tk, lam_init, q_scale):
    if has_lat:
        (q_ref, kc_ref, vc_ref, kl_ref, vl_ref, lamp_ref, nw_ref, o_ref, m_ref, acc_ref, vtc_ref, vtl_ref,
         st_ref, mx_ref) = refs
    else:
        q_ref, kc_ref, vc_ref, lamp_ref, nw_ref, o_ref, m_ref, acc_ref, vtc_ref = refs
    tq = q_ref.shape[0]
    eye = (lax.broadcasted_iota(jnp.int32, (HEAD_W, HEAD_W), 0)
           == lax.broadcasted_iota(jnp.int32, (HEAD_W, HEAD_W), 1)).astype(_F32).astype(_BF16)

    @pl.when(pl.program_id(2) == 0)
    def _():
        ones = jnp.ones((ATT_ONES_ROWS, vtc_ref.shape[1]), _BF16)
        vtc_ref[0:HEAD_W, :] = _dot_nt(eye, vc_ref[...]).astype(_BF16)
        vtc_ref[HEAD_W:, :] = ones
        if has_lat:
            def tbody(i, carry):
                off = pl.multiple_of(i * tk, tk)
                vtl_ref[0:HEAD_W, pl.ds(off, tk)] = _dot_nt(eye, vl_ref[pl.ds(off, tk), :]).astype(_BF16)
                vtl_ref[HEAD_W:, pl.ds(off, tk)] = jnp.ones((ATT_ONES_ROWS, tk), _BF16)
                return carry
            lax.fori_loop(0, vl_ref.shape[0] // tk, tbody, 0)

    q = q_ref[...].astype(_F32)
    if q_scale is not None:
        q = q * q_scale
    lane = lax.broadcasted_iota(jnp.int32, q.shape, 1)
    qq = jnp.concatenate([jnp.where(lane < DA_HEAD_DIM, q, 0.0), jnp.where(lane >= DA_HEAD_DIM, q, 0.0)],
                         axis=0).astype(_BF16)

    m_ref[...] = jnp.full_like(m_ref, -jnp.inf)
    acc_ref[...] = jnp.zeros_like(acc_ref)

    def softmax_pv(st, st_max, vt):
        group = min(ATT_Q_GROUP, 2 * tq)
        for c0 in range(0, 2 * tq, group):
            cs = slice(c0, c0 + group)
            m_prev = m_ref[:, cs]
            m_new = jnp.maximum(m_prev, st_max[:, cs])
            alpha = jnp.exp2(m_prev - m_new)
            pt = jnp.exp2(st[:, cs] - m_new).astype(_BF16)
            acc_ref[:, cs] = alpha * acc_ref[:, cs] + _dot(vt, pt)
            m_ref[:, cs] = m_new

    def scores(t, buf):
        off = pl.multiple_of((t - 1) * tk, tk)
        st = _dot_nt(kl_ref[pl.ds(off, tk), :], qq)
        st_ref[buf] = st
        mx_ref[buf] = jnp.max(st, axis=0, keepdims=True)

    st_c = _dot_nt(kc_ref[...], qq)
    mx_c = jnp.max(st_c, axis=0, keepdims=True)
    if not has_lat:
        softmax_pv(st_c, mx_c, vtc_ref[...])
    else:
        n = kl_ref.shape[0] // tk

        def consume(t, buf):
            off = pl.multiple_of((t - 1) * tk, tk)
            softmax_pv(st_ref[buf], mx_ref[buf], vtl_ref[:, pl.ds(off, tk)])

        scores(1, 1)
        softmax_pv(st_c, mx_c, vtc_ref[...])
        n_pairs = (n - 1) // 2

        def body(j, carry):
            t = 2 * j + 1
            scores(t + 1, 0)
            consume(t, 1)
            scores(t + 2, 1)
            consume(t + 1, 0)
            return carry
        lax.fori_loop(0, n_pairs, body, 0)
        for t in range(2 * n_pairs + 1, n + 1):
            if t < n:
                scores(t + 1, (t + 1) % 2)
            consume(t, t % 2)

    lp = lamp_ref[...]
    lam = (jnp.exp(jnp.sum(lp[0:1] * lp[1:2], axis=-1, keepdims=True))
           - jnp.exp(jnp.sum(lp[2:3] * lp[3:4], axis=-1, keepdims=True)) + lam_init)
    on = acc_ref[0:HEAD_W, :] / acc_ref[HEAD_W:HEAD_W + 1, :]
    ot = on[:, 0:tq] - lam * on[:, tq:2 * tq]
    ms = jnp.mean(ot * ot, axis=0, keepdims=True)
    o = (ot * lax.rsqrt(ms + EPS)).T
    o_ref[...] = (o * nw_ref[...] * (1.0 - lam_init)).astype(o_ref.dtype)


def _attn(qkv, tq, nq, p, b, s, c, mix_w, lam_p, nw, layer, lam_init, has_lat, tk=1024):
    nheads = mix_w // HEAD_W
    aq0, ak0, av0 = 8 * mix_w // HEAD_W, 9 * mix_w // HEAD_W, 10 * mix_w // HEAD_W
    ctx_blk0 = b * s // c
    if has_lat:
        q_spec = pl.BlockSpec((None, tq, HEAD_W), lambda bi, h, qi: (h, bi * nq + qi, 0))
    else:
        q_spec = pl.BlockSpec((tq, HEAD_W), lambda bi, h, qi: (b * s // tq + bi * nq + qi, aq0 + h))
    in_specs = [
        q_spec,
        pl.BlockSpec((c, HEAD_W), lambda bi, h, qi: (ctx_blk0 + bi, ak0 + h)),
        pl.BlockSpec((c, HEAD_W), lambda bi, h, qi: (ctx_blk0 + bi, av0 + h)),
    ]
    args = [qkv if has_lat else p, p, p]
    if has_lat:
        in_specs += [
            pl.BlockSpec((None, s, HEAD_W), lambda bi, h, qi: (nheads + h, bi, 0)),
            pl.BlockSpec((None, s, HEAD_W), lambda bi, h, qi: (2 * nheads + h, bi, 0)),
        ]
        args += [qkv, qkv]
    in_specs += [
        pl.BlockSpec((None, 4, DA_HEAD_DIM), lambda bi, h, qi: (layer, 0, 0)),
        pl.BlockSpec((None, 1, HEAD_W), lambda bi, h, qi: (layer, 0, 0)),
    ]
    args += [lam_p, nw.reshape(-1, 1, HEAD_W)]
    vt_rows = HEAD_W + ATT_ONES_ROWS
    scratch = [pltpu.VMEM((1, 2 * tq), _F32), pltpu.VMEM((vt_rows, 2 * tq), _F32), pltpu.VMEM((vt_rows, c), _BF16)]
    tk = min(tk, s)
    if has_lat:
        scratch += [pltpu.VMEM((vt_rows, s), _BF16), pltpu.VMEM((2, tk, 2 * tq), _F32),
                    pltpu.VMEM((2, 1, 2 * tq), _F32)]
    return pl.pallas_call(
        functools.partial(_attn_kernel, has_lat=has_lat, tk=tk, lam_init=lam_init,
                          q_scale=None if has_lat else ATT_Q_SCALE),
        grid=(b, nheads, nq),
        in_specs=in_specs,
        out_specs=pl.BlockSpec((tq, HEAD_W), lambda bi, h, qi: (bi * nq + qi, h)),
        out_shape=jax.ShapeDtypeStruct((b * nq * tq, mix_w), _BF16),
        scratch_shapes=scratch,
        compiler_params=_params(("arbitrary", "arbitrary", "arbitrary")),
        name="diff_attn_lat" if has_lat else "diff_attn_ctx",
    )(*args)


def _branch_kernel(of_ref, ob_ref, hg_ref, cb_ref, cc_ref, cu_ref, ccp_ref, cup_ref, ccn_ref, cun_ref,
                   hnw_ref, cw_ref, a_ref, b_ref, *, n_lat, s, c):
    tm = of_ref.shape[0]
    nw = hnw_ref[...]
    for h in range(of_ref.shape[1] // HEAD_W):
        sl = slice(h * HEAD_W, (h + 1) * HEAD_W)
        o = of_ref[:, sl] + ob_ref[:, sl]
        ms = jnp.mean(o * o, axis=-1, keepdims=True)
        g = hg_ref[:, sl].astype(_F32)
        a_ref[:, sl] = ((o * lax.rsqrt(ms + EPS) * nw) * (g * jax.nn.sigmoid(g))).astype(a_ref.dtype)
    row = pl.program_id(0) * tm + lax.broadcasted_iota(jnp.int32, (tm, 1), 0)
    pos = jnp.where(row < n_lat, row % s, (row - n_lat) % c)
    seq_len = jnp.where(row < n_lat, s, c)
    local = lax.broadcasted_iota(jnp.int32, (tm, 1), 0)
    x = cc_ref[...].astype(_F32) * cu_ref[...].astype(_F32)
    x_halo_p = ccp_ref[7:8, :].astype(_F32) * cup_ref[7:8, :].astype(_F32)
    x_halo_n = ccn_ref[0:1, :].astype(_F32) * cun_ref[0:1, :].astype(_F32)
    x_prev = jnp.where(local == 0, x_halo_p, pltpu.roll(x, 1, axis=0))
    x_prev = jnp.where(pos == 0, 0.0, x_prev)
    x_next = jnp.where(local == tm - 1, x_halo_n, pltpu.roll(x, tm - 1, axis=0))
    x_next = jnp.where(pos == seq_len - 1, 0.0, x_next)
    y = cw_ref[0:1, :] * x_prev + cw_ref[1:2, :] * x + cw_ref[2:3, :] * x_next
    b_ref[...] = (cb_ref[...].astype(_F32) * y).astype(b_ref.dtype)


def _branches(o_fb, p, n_rows, hg_nw, conv_w, layer, b, s, c, mix_w, tm=256):
    tm = math.gcd(math.gcd(tm, s), b * c)
    last8 = p.shape[0] // 8 - 1
    main = lambda col: pl.BlockSpec((tm, mix_w), lambda i: (i, col))
    prev = lambda col: pl.BlockSpec((8, mix_w), lambda i: (jnp.maximum(i * (tm // 8) - 1, 0), col))
    nxt = lambda col: pl.BlockSpec((8, mix_w), lambda i: (jnp.minimum((i + 1) * (tm // 8), last8), col))
    out_sds = jax.ShapeDtypeStruct((n_rows, mix_w), _BF16)
    return pl.pallas_call(
        functools.partial(_branch_kernel, n_lat=b * s, s=s, c=c),
        grid=(n_rows // tm,),
        in_specs=[
            pl.BlockSpec((None, tm, mix_w), lambda i: (0, i, 0)),
            pl.BlockSpec((None, tm, mix_w), lambda i: (1, i, 0)),
            main(4), main(5), main(6), main(7), prev(6), prev(7), nxt(6), nxt(7),
            pl.BlockSpec((None, 1, HEAD_W), lambda i: (layer, 0, 0)),
            pl.BlockSpec((None, 3, mix_w), lambda i: (layer, 0, 0)),
        ],
        out_specs=[pl.BlockSpec((tm, mix_w), lambda i: (i, 0)), pl.BlockSpec((tm, mix_w), lambda i: (i, 0))],
        out_shape=[out_sds, out_sds],
        compiler_params=_params(("parallel",)),
        name="branch_prep",
    )(o_fb, o_fb, p, p, p, p, p, p, p, p, hg_nw.reshape(-1, 1, HEAD_W), conv_w)


def _merge_kernel(h_ref, mod_ref, a_ref, b_ref, c_ref, ga_ref, gb_ref, gc_ref, wa_ref, wb_ref, wc_ref,
                  wo_ref, o_ref):
    j = pl.program_id(1)

    @pl.when(j == 0)
    def _():
        o_ref[...] = jnp.zeros_like(o_ref)

    def gated(g_ref, x_ref, w_ref):
        return jax.nn.sigmoid(g_ref[...].astype(_F32)) * _dot(x_ref[...], w_ref[...])

    y = gated(ga_ref, a_ref, wa_ref) + gated(gb_ref, b_ref, wb_ref) + gated(gc_ref, c_ref, wc_ref)
    o_ref[...] += _dot(y.astype(_BF16), wo_ref[...])

    @pl.when(j == pl.num_programs(1) - 1)
    def _():
        o_ref[...] = h_ref[...] + mod_ref[5:6, :] * o_ref[...]


def _merge(h, n_rows, mod_l, a, bb, cc, p, w_branch, w_out, layer, dims, mix_w, tn=512):
    d = h.shape[1]
    tm = dims["tm"]
    mod_row = dims["mod_row"]
    g0 = 11 * mix_w // tn
    gstep = d // tn
    br = lambda: pl.BlockSpec((tm, mix_w), lambda i, j: (i, 0))
    gate = lambda k: pl.BlockSpec((tm, tn), lambda i, j: (i, g0 + k * gstep + j))
    wbr = lambda k: pl.BlockSpec((None, None, mix_w, tn), lambda i, j: (layer, k, 0, j))
    return pl.pallas_call(
        _merge_kernel,
        grid=(pl.cdiv(n_rows, tm), d // tn),
        in_specs=[
            pl.BlockSpec((tm, d), lambda i, j: (i, 0)),
            pl.BlockSpec((None, N_MOD, d), lambda i, j: (mod_row(i), 0, 0)),
            br(), br(), br(), gate(0), gate(1), gate(2), wbr(0), wbr(1), wbr(2),
            pl.BlockSpec((None, tn, d), lambda i, j: (layer, j, 0)),
        ],
        out_specs=pl.BlockSpec((tm, d), lambda i, j: (i, 0)),
        out_shape=jax.ShapeDtypeStruct((n_rows, d), _F32),
        compiler_params=_params(("parallel", "arbitrary")),
        name="merge_out",
    )(h, mod_l, a, bb, cc, p, p, p, w_branch, w_branch, w_branch, w_out)


def kernel(x, c, ctx, c_ctx, w_ada, b_ada, norm_w, ffn_w13, ffn_w2, w_in, hgrn_lb, hgrn_norm_w, conv_w,
           da_lambda, da_norm_w, w_branch, w_out, final_norm_w):
    b, s, d = x.shape
    cl = ctx.shape[1]
    depth = w_ada.shape[0]
    mix_w = d // 2
    n_lat, n_ctx = b * s, b * cl
    n_all = n_lat + n_ctx
    tm = math.gcd(math.gcd(512, s), n_ctx)
    dims = {
        "tm": tm,
        "tm_big": math.gcd(1024, s),
        "s": s,
        "b": b,
        "mod_row": lambda i: jnp.minimum(i * tm // s, b),
        "norm_w": norm_w.reshape(depth * 3, 1, d),
    }

    w13_b, w2_b, w_in_b = ffn_w13.astype(_BF16), ffn_w2.astype(_BF16), w_in.astype(_BF16)
    w_branch_b, w_out_b = w_branch.astype(_BF16), w_out.astype(_BF16)

    cvec = jnp.concatenate([c, c_ctx[None, :], jnp.zeros((8 - b - 1, d), _F32)], axis=0)
    mod = _ada_mod(cvec, w_ada, b_ada).reshape(depth, 8, N_MOD, d)

    cum = jnp.cumsum(jax.nn.softmax(hgrn_lb.astype(_F32), axis=1), axis=1)
    lb = cum - cum[:, :1]
    lb_tab = jnp.stack([jnp.log(lb), jnp.log1p(-lb), 1.0 - lb], axis=2)
    w_tab, lmap = _hgrn_tables()
    cos, sin = _rope_tables(s)

    h = jnp.concatenate([x.reshape(n_lat, d), ctx.reshape(n_ctx, d)], axis=0)
    for layer in range(depth):
        last = layer == depth - 1
        lam_init = 0.8 - 0.6 * math.exp(-0.3 * layer)
        mod_l = mod[layer]
        h = _ffn(h, n_all, mod_l, layer * 3, w13_b, w2_b, layer, 0, 0, dims)
        p = _mixer_in(h, mod_l, layer * 3 + 1, w_in_b, layer, dims)
        qkv = _rope(p, cos, sin, b, s, 8 * mix_w)
        o_fb = _hgrn(p, lb_tab[:, layer], w_tab, lmap, b, s, cl, mix_w)
        tq = min(512, s)
        c_att = _attn(qkv, tq, s // tq, p, b, s, cl, mix_w, da_lambda, da_norm_w, layer, lam_init, True)
        n_rows = n_lat if last else n_all
        if not last:
            c_ctx_att = _attn(None, cl, 1, p, b, s, cl, mix_w, da_lambda, da_norm_w, layer, lam_init, False)
            c_att = jnp.concatenate([c_att, c_ctx_att], axis=0)
        a_br, b_br = _branches(o_fb, p, n_rows, hgrn_norm_w, conv_w, layer, b, s, cl, mix_w)
        h = _merge(h, n_rows, mod_l, a_br, b_br, c_att, p, w_branch_b, w_out_b, layer, dims, mix_w)
        h = _ffn(h, n_rows, mod_l, layer * 3 + 2, w13_b, w2_b, layer, 1, 6, dims,
                 final_w=final_norm_w if last else None)
    return h.reshape(b, s, d)
```

```python
import functools
import math

import jax
import jax.numpy as jnp
import numpy as np
from jax import lax
from jax.experimental import pallas as pl
from jax.experimental.pallas import tpu as pltpu

_F32 = jnp.float32
_BF16 = jnp.bfloat16

EPS = 1e-6
N_MOD = 9
GRID_W = 64
ROPE_THETA = 10000.0
HEAD_W = 128
DA_HEAD_DIM = 64
LOG2_E = math.log2(math.e)
ATT_Q_SCALE = LOG2_E * DA_HEAD_DIM ** -0.5
ATT_Q_GROUP = 1024
ATT_ONES_ROWS = 16
HG_CHUNK = 64
HG_LEVELS = 6
HG_HEAD_GROUP = 8
ROW_CHUNK = 128
V7X_VMEM_LIMIT = 56 * 1024 * 1024
V7X_VMEM_LIMIT_FFN = 60 * 1024 * 1024


def _dot(a, b):
    return jnp.dot(a, b, preferred_element_type=_F32)


def _dot_nt(a, b):
    return lax.dot_general(a, b, (((1,), (1,)), ((), ())), preferred_element_type=_F32)


def _dot_tn(a, b):
    return lax.dot_general(a, b, (((0,), (0,)), ((), ())), preferred_element_type=_F32)


def _params(semantics, vmem=V7X_VMEM_LIMIT):
    return pltpu.CompilerParams(dimension_semantics=semantics, vmem_limit_bytes=vmem)


def _for_row_chunks(n_rows, body):
    def step(c, carry):
        body(pl.multiple_of(c * ROW_CHUNK, ROW_CHUNK))
        return carry
    lax.fori_loop(0, n_rows // ROW_CHUNK, step, 0)


def _norm_mod(h, nw, shift, scale):
    ms = jnp.mean(h * h, axis=-1, keepdims=True)
    return (h * lax.rsqrt(ms + EPS) * nw) * (1.0 + scale) + shift


def _ada_kernel(c_ref, w_ref, b_ref, o_ref):
    c = c_ref[...]
    s = (c * jax.nn.sigmoid(c)).astype(_BF16)
    o_ref[...] = _dot(s, w_ref[...].astype(_BF16)) + b_ref[...]


def _ada_mod(cvec, w_ada, b_ada, tn=1024):
    depth, d, nd = w_ada.shape
    rows = cvec.shape[0]
    return pl.pallas_call(
        _ada_kernel,
        grid=(depth, nd // tn),
        in_specs=[
            pl.BlockSpec((rows, d), lambda l, j: (0, 0)),
            pl.BlockSpec((None, d, tn), lambda l, j: (l, 0, j)),
            pl.BlockSpec((None, 1, tn), lambda l, j: (l, 0, j)),
        ],
        out_specs=pl.BlockSpec((None, rows, tn), lambda l, j: (l, 0, j)),
        out_shape=jax.ShapeDtypeStruct((depth, rows, nd), _F32),
        compiler_params=_params(("arbitrary", "arbitrary")),
        name="ada_mod",
    )(cvec, w_ada, b_ada.reshape(depth, 1, nd))


def _on_row_tiles(tail_rows, tm, body):
    if tail_rows == 0:
        body(tm)
        return
    is_tail = pl.program_id(0) == pl.num_programs(0) - 1

    @pl.when(jnp.logical_not(is_tail))
    def _():
        body(tm)

    @pl.when(is_tail)
    def _():
        body(tail_rows)


def _ffn_kernel(h_ref, mod_ref, nw_ref, w1_ref, w3_ref, w2_ref, *rest, mod_off, final_norm, tail_rows):
    if final_norm:
        fw_ref, o_ref, xn_ref = rest
    else:
        o_ref, xn_ref = rest
    j = pl.program_id(1)

    def body(n_rows):
        @pl.when(j == 0)
        def _():
            def rows(r0):
                rs = pl.ds(r0, ROW_CHUNK)
                xn = _norm_mod(h_ref[rs, :], nw_ref[...], mod_ref[mod_off:mod_off + 1, :],
                               mod_ref[mod_off + 1:mod_off + 2, :])
                xn_ref[rs, :] = xn.astype(_BF16)
                o_ref[rs, :] = jnp.zeros((ROW_CHUNK, o_ref.shape[1]), o_ref.dtype)
            _for_row_chunks(n_rows, rows)

        xn = xn_ref[0:n_rows, :]
        g = _dot(xn, w1_ref[...])
        u = _dot(xn, w3_ref[...])
        a = (g * jax.nn.sigmoid(g) * u).astype(_BF16)
        o_ref[0:n_rows, :] += _dot(a, w2_ref[...])

        @pl.when(j == pl.num_programs(1) - 1)
        def _():
            gate = mod_ref[mod_off + 2:mod_off + 3, :]

            def rows(r0):
                rs = pl.ds(r0, ROW_CHUNK)
                out = h_ref[rs, :] + 0.5 * gate * o_ref[rs, :]
                if final_norm:
                    ms = jnp.mean(out * out, axis=-1, keepdims=True)
                    out = out * lax.rsqrt(ms + EPS) * fw_ref[...]
                o_ref[rs, :] = out
            _for_row_chunks(n_rows, rows)

    _on_row_tiles(tail_rows, h_ref.shape[0], body)


def _ffn(h, n_rows, mod_l, nw, w13, w2, layer, which, mod_off, dims, final_w=None, tf=512):
    d = h.shape[1]
    f = w2.shape[2]
    tm = dims["tm_big"]
    nfb = f // tf
    mod_row = lambda i: jnp.minimum(i * tm // dims["s"], dims["b"])
    in_specs = [
        pl.BlockSpec((tm, d), lambda i, j: (i, 0)),
        pl.BlockSpec((None, N_MOD, d), lambda i, j: (mod_row(i), 0, 0)),
        pl.BlockSpec((None, 1, d), lambda i, j: (nw, 0, 0)),
        pl.BlockSpec((None, None, d, tf), lambda i, j: (layer, which, 0, j)),
        pl.BlockSpec((None, None, d, tf), lambda i, j: (layer, which, 0, j + nfb)),
        pl.BlockSpec((None, None, tf, d), lambda i, j: (layer, which, j, 0)),
    ]
    args = [h, mod_l, dims["norm_w"], w13, w13, w2]
    if final_w is not None:
        in_specs.append(pl.BlockSpec((1, d), lambda i, j: (0, 0)))
        args.append(final_w.reshape(1, d))
    return pl.pallas_call(
        functools.partial(_ffn_kernel, mod_off=mod_off, final_norm=final_w is not None, tail_rows=n_rows % tm),
        grid=(pl.cdiv(n_rows, tm), nfb),
        in_specs=in_specs,
        out_specs=pl.BlockSpec((tm, d), lambda i, j: (i, 0)),
        out_shape=jax.ShapeDtypeStruct((n_rows, d), _F32),
        scratch_shapes=[pltpu.VMEM((tm, d), _BF16)],
        compiler_params=_params(("parallel", "arbitrary"), vmem=V7X_VMEM_LIMIT_FFN),
        name="half_ffn",
    )(*args)


def _proj_kernel(h_ref, mod_ref, nw_ref, w_ref, o_ref, xn_ref, *, tail_rows):
    def body(n_rows):
        @pl.when(pl.program_id(1) == 0)
        def _():
            xn = _norm_mod(h_ref[0:n_rows, :], nw_ref[...], mod_ref[3:4, :], mod_ref[4:5, :])
            xn_ref[0:n_rows, :] = xn.astype(_BF16)

        o_ref[0:n_rows, :] = _dot(xn_ref[0:n_rows, :], w_ref[...]).astype(o_ref.dtype)

    _on_row_tiles(tail_rows, h_ref.shape[0], body)


def _mixer_in(h, mod_l, nw, w_in, layer, dims, tn=1024):
    n, d = h.shape
    cols = w_in.shape[2]
    tm = dims["tm_big"]
    mod_row = lambda i: jnp.minimum(i * tm // dims["s"], dims["b"])
    return pl.pallas_call(
        functools.partial(_proj_kernel, tail_rows=n % tm),
        grid=(pl.cdiv(n, tm), cols // tn),
        in_specs=[
            pl.BlockSpec((tm, d), lambda i, j: (i, 0)),
            pl.BlockSpec((None, N_MOD, d), lambda i, j: (mod_row(i), 0, 0)),
            pl.BlockSpec((None, 1, d), lambda i, j: (nw, 0, 0)),
            pl.BlockSpec((None, d, tn), lambda i, j: (layer, 0, j)),
        ],
        out_specs=pl.BlockSpec((tm, tn), lambda i, j: (i, j)),
        out_shape=jax.ShapeDtypeStruct((n, cols), _BF16),
        scratch_shapes=[pltpu.VMEM((tm, d), _BF16)],
        compiler_params=_params(("parallel", "arbitrary")),
        name="mixer_in",
    )(h, mod_l, dims["norm_w"], w_in)


def _rope_kernel(x_ref, cos_ref, sin_ref, o_ref):
    cos = cos_ref[...]
    sin = sin_ref[...]
    lane = lax.broadcasted_iota(jnp.int32, cos.shape, 1)
    first = (lane % 32) < 16
    n_groups = x_ref.shape[1] // HEAD_W
    for g in range(n_groups):
        sl = slice(g * HEAD_W, (g + 1) * HEAD_W)
        x = x_ref[:, sl].astype(_F32)
        rot = jnp.where(first, pltpu.roll(x, HEAD_W - 16, axis=1), pltpu.roll(x, 16, axis=1))
        y = x * cos + rot * sin
        if g < n_groups // 2:
            y = y * ATT_Q_SCALE
        o_ref[:, sl] = y.astype(o_ref.dtype)


def _rope_tables(n_tokens):
    rows = n_tokens // GRID_W
    row = jnp.repeat(jnp.arange(rows, dtype=jnp.int32), GRID_W)
    col = jnp.tile(jnp.arange(GRID_W, dtype=jnp.int32), rows)
    half = DA_HEAD_DIM // 2
    inv_freq = ROPE_THETA ** (-jnp.arange(0, half, 2, dtype=_F32) / half)
    ang_r = row.astype(_F32)[:, None] * inv_freq
    ang_c = col.astype(_F32)[:, None] * inv_freq
    ang = jnp.concatenate([ang_r, ang_r, ang_c, ang_c], axis=-1)
    ang = jnp.concatenate([ang, ang], axis=-1)
    sign = jnp.where((jnp.arange(HEAD_W) % 32) < 16, -1.0, 1.0)
    return jnp.cos(ang), jnp.sin(ang) * sign


def _rope(p, cos, sin, b, s, col0, tr=512):
    width = 2 * dims_mix_w(p)
    tr = min(tr, s)
    return pl.pallas_call(
        _rope_kernel,
        grid=(b * s // tr,),
        in_specs=[
            pl.BlockSpec((tr, width), lambda i: (i, col0 // width)),
            pl.BlockSpec((tr, HEAD_W), lambda i: (i % (s // tr), 0)),
            pl.BlockSpec((tr, HEAD_W), lambda i: (i % (s // tr), 0)),
        ],
        out_specs=pl.BlockSpec((tr, width), lambda i: (i, 0)),
        out_shape=jax.ShapeDtypeStruct((b * s, width), _BF16),
        compiler_params=_params(("parallel",)),
        name="axial_rope",
    )(p, cos, sin)


def dims_mix_w(p):
    return p.shape[1] // 17


def _hgrn_tables():
    L = HG_CHUNK
    w = np.zeros((2, (HG_LEVELS + 2) * L + 8, L), np.float32)
    lmap = np.full((2, L, L), HG_LEVELS + 1, np.int32)
    r = np.arange(L)
    for t in range(L):
        w[0, t, r <= t] = 1
        w[0, L + t, r > t] = 1
        w[1, t, r >= t] = 1
        w[1, L + t, r < t] = 1
        lmap[:, t, t] = 0
        for lvl in range(1, HG_LEVELS + 1):
            bs = (2 * L) >> lvl
            mid = (t // bs) * bs + bs // 2
            base = (lvl + 1) * L + t
            if t >= mid:
                w[0, base, (r >= mid) & (r <= t)] = 1
                w[1, base, (r >= mid) & (r < t)] = 1
            else:
                w[0, base, (r > t) & (r < mid)] = 1
                w[1, base, (r >= t) & (r < mid)] = 1
            for s_ in range(L):
                if s_ // bs == t // bs and t >= mid and s_ < mid:
                    lmap[0, t, s_] = lvl
                    lmap[1, s_, t] = lvl
    w[:, (HG_LEVELS + 2) * L:, :] = 1
    w = np.concatenate([w, w], axis=2)
    return jnp.asarray(w, _BF16), jnp.asarray(lmap)


def _hgrn_kernel(q_ref, z_ref, v_ref, lb_ref, w_ref, lmap_ref, o_ref, st_ref):
    L = HG_CHUNK

    @pl.when(pl.program_id(2) == 0)
    def _():
        st_ref[...] = jnp.zeros_like(st_ref)

    n_heads = q_ref.shape[1] // HEAD_W
    lmap = lmap_ref[...]
    for h0 in range(0, n_heads, HG_HEAD_GROUP):
        gs = slice(h0 * HEAD_W, (h0 + HG_HEAD_GROUP) * HEAD_W)
        heads = [slice(h * HEAD_W, (h + 1) * HEAD_W) for h in range(HG_HEAD_GROUP)]
        z = z_ref[:, gs].astype(_F32)
        ls = jnp.minimum(z, 0.0) - jnp.log(1.0 + jnp.exp(-jnp.abs(z)))
        bb = lb_ref[1:2, gs] + ls
        log_lb = lb_ref[0:1, gs]
        lf = jnp.maximum(log_lb, bb) + jnp.log(1.0 + jnp.exp(-jnp.abs(log_lb - bb)))
        k = lb_ref[2:3, gs] * jnp.exp(ls - z)
        lf2 = lf * LOG2_E
        lf_hi = lf2.astype(_BF16)
        lf_lo = (lf2 - lf_hi.astype(_F32)).astype(_BF16)
        e = jnp.exp2(_dot(w_ref[...], jnp.concatenate([lf_hi, lf_lo], axis=0)))
        q = q_ref[:, gs].astype(_F32)
        v = v_ref[:, gs]
        qk = q * k
        a = [jnp.where(lmap == 0, jnp.sum(qk[:, sl], axis=-1, keepdims=True) * jnp.ones((1, L), _F32), 0.0)
             for sl in heads]
        for lvl in range(1, HG_LEVELS + 1):
            el = e[(lvl + 1) * L:(lvl + 2) * L]
            ql = (q * el).astype(_BF16)
            kl = (k * el).astype(_BF16)
            a = [jnp.where(lmap == lvl, _dot_nt(ql[:, sl], kl[:, sl]), a[h]) for h, sl in enumerate(heads)]
        q_in = (q * e[0:L]).astype(_BF16)
        k_out = (k * e[L:2 * L]).astype(_BF16)
        tot = e[(HG_LEVELS + 2) * L:(HG_LEVELS + 2) * L + 1]
        for h, sl in enumerate(heads):
            st = st_ref[h0 + h]
            osl = slice((h0 + h) * HEAD_W, (h0 + h + 1) * HEAD_W)
            o_ref[:, osl] = _dot(a[h].astype(_BF16), v[:, sl]) + _dot_nt(q_in[:, sl], st.astype(_BF16))
            st_ref[h0 + h] = tot[:, sl] * st + _dot_tn(v[:, sl], k_out[:, sl])


def _hgrn(p, lb_tab, w_tab, lmap, b, s, c, mix_w):
    n = p.shape[0]
    L = HG_CHUNK
    ncc, ncl = c // L, s // L
    nheads = mix_w // HEAD_W

    def row_block(bi, di, j):
        is_ctx = j < ncc
        cj = jnp.where(di == 0, j, ncc - 1 - j)
        lj = jnp.where(di == 0, j - ncc, ncl - 1 - (j - ncc))
        return jnp.where(is_ctx, (b * s + bi * c) // L + cj, bi * ncl + lj)

    def spec(col):
        return pl.BlockSpec((L, mix_w), lambda bi, di, j: (row_block(bi, di, j), col))

    return pl.pallas_call(
        _hgrn_kernel,
        grid=(b, 2, ncc + ncl),
        in_specs=[
            spec(0),
            pl.BlockSpec((L, mix_w), lambda bi, di, j: (row_block(bi, di, j), 1 + di)),
            spec(3),
            pl.BlockSpec((None, 3, mix_w), lambda bi, di, j: (di, 0, 0)),
            pl.BlockSpec((None,) + w_tab.shape[1:], lambda bi, di, j: (di, 0, 0)),
            pl.BlockSpec((None, L, L), lambda bi, di, j: (di, 0, 0)),
        ],
        out_specs=pl.BlockSpec((None, L, mix_w), lambda bi, di, j: (di, row_block(bi, di, j), 0)),
        out_shape=jax.ShapeDtypeStruct((2, n, mix_w), _F32),
        scratch_shapes=[pltpu.VMEM((nheads, HEAD_W, HEAD_W), _F32)],
        compiler_params=_params(("arbitrary", "arbitrary", "arbitrary")),
        name="hgrn_scan",
    )(p, p, p, lb_tab, w_tab, lmap)


def _attn_kernel(*refs, has_lat, tk, lam_init, q_scale):
    if has_lat:
        (q_ref, kc_ref, vc_ref, kl_ref, vl_ref, lamp_ref, nw_ref, o_ref, m_ref, acc_ref, vtc_ref, vtl_ref,
         st_ref, mx_ref) = refs
    else:
        q_ref, kc_ref, vc_ref, lamp_ref, nw_ref, o_ref, m_ref, acc_ref, vtc_ref = refs
    tq = q_ref.shape[0]
    eye = (lax.broadcasted_iota(jnp.int32, (HEAD_W, HEAD_W), 0)
           == lax.broadcasted_iota(jnp.int32, (HEAD_W, HEAD_W), 1)).astype(_F32).astype(_BF16)

    @pl.when(pl.program_id(2) == 0)
    def _():
        ones = jnp.ones((ATT_ONES_ROWS, vtc_ref.shape[1]), _BF16)
        vtc_ref[0:HEAD_W, :] = _dot_nt(eye, vc_ref[...]).astype(_BF16)
        vtc_ref[HEAD_W:, :] = ones
        if has_lat:
            def tbody(i, carry):
                off = pl.multiple_of(i * tk, tk)
                vtl_ref[0:HEAD_W, pl.ds(off, tk)] = _dot_nt(eye, vl_ref[pl.ds(off, tk), :]).astype(_BF16)
                vtl_ref[HEAD_W:, pl.ds(off, tk)] = jnp.ones((ATT_ONES_ROWS, tk), _BF16)
                return carry
            lax.fori_loop(0, vl_ref.shape[0] // tk, tbody, 0)

    q = q_ref[...].astype(_F32)
    if q_scale is not None:
        q = q * q_scale
    lane = lax.broadcasted_iota(jnp.int32, q.shape, 1)
    qq = jnp.concatenate([jnp.where(lane < DA_HEAD_DIM, q, 0.0), jnp.where(lane >= DA_HEAD_DIM, q, 0.0)],
                         axis=0).astype(_BF16)

    m_ref[...] = jnp.full_like(m_ref, -jnp.inf)
    acc_ref[...] = jnp.zeros_like(acc_ref)

    def softmax_pv(st, st_max, vt):
        group = min(ATT_Q_GROUP, 2 * tq)
        for c0 in range(0, 2 * tq, group):
            cs = slice(c0, c0 + group)
            m_prev = m_ref[:, cs]
            m_new = jnp.maximum(m_prev, st_max[:, cs])
            alpha = jnp.exp2(m_prev - m_new)
            pt = jnp.exp2(st[:, cs] - m_new).astype(_BF16)
            acc_ref[:, cs] = alpha * acc_ref[:, cs] + _dot(vt, pt)
            m_ref[:, cs] = m_new

    def scores(t, buf):
        off = pl.multiple_of((t - 1) * tk, tk)
        st = _dot_nt(kl_ref[pl.ds(off, tk), :], qq)
        st_ref[buf] = st
        mx_ref[buf] = jnp.max(st, axis=0, keepdims=True)

    st_c = _dot_nt(kc_ref[...], qq)
    mx_c = jnp.max(st_c, axis=0, keepdims=True)
    if not has_lat:
        softmax_pv(st_c, mx_c, vtc_ref[...])
    else:
        n = kl_ref.shape[0] // tk

        def consume(t, buf):
            off = pl.multiple_of((t - 1) * tk, tk)
            softmax_pv(st_ref[buf], mx_ref[buf], vtl_ref[:, pl.ds(off, tk)])

        scores(1, 1)
        softmax_pv(st_c, mx_c, vtc_ref[...])
        n_pairs = (n - 1) // 2

        def body(j, carry):
            t = 2 * j + 1
            scores(t + 1, 0)
            consume(t, 1)
            scores(t + 2, 1)
            consume(t + 1, 0)
            return carry
        lax.fori_loop(0, n_pairs, body, 0)
        for t in range(2 * n_pairs + 1, n + 1):
            if t < n:
                scores(t + 1, (t + 1) % 2)
            consume(t, t % 2)

    lp = lamp_ref[...]
    lam = (jnp.exp(jnp.sum(lp[0:1] * lp[1:2], axis=-1, keepdims=True))
           - jnp.exp(jnp.sum(lp[2:3] * lp[3:4], axis=-1, keepdims=True)) + lam_init)
    on = acc_ref[0:HEAD_W, :] / acc_ref[HEAD_W:HEAD_W + 1, :]
    ot = on[:, 0:tq] - lam * on[:, tq:2 * tq]
    ms = jnp.mean(ot * ot, axis=0, keepdims=True)
    o = (ot * lax.rsqrt(ms + EPS)).T
    o_ref[...] = (o * nw_ref[...] * (1.0 - lam_init)).astype(o_ref.dtype)


def _attn(q_arr, q_col0, q_row0, tq, nq, p, kl_arr, b, s, c, mix_w, lam_p, nw, layer, lam_init, has_lat, tk=1024):
    nheads = mix_w // HEAD_W
    ak0, av0 = 9 * mix_w // HEAD_W, 10 * mix_w // HEAD_W
    ctx_blk0 = b * s // c
    in_specs = [
        pl.BlockSpec((tq, HEAD_W), lambda bi, h, qi: (q_row0 // tq + bi * nq + qi, q_col0 // HEAD_W + h)),
        pl.BlockSpec((c, HEAD_W), lambda bi, h, qi: (ctx_blk0 + bi, ak0 + h)),
        pl.BlockSpec((c, HEAD_W), lambda bi, h, qi: (ctx_blk0 + bi, av0 + h)),
    ]
    args = [q_arr, p, p]
    if has_lat:
        in_specs += [
            pl.BlockSpec((s, HEAD_W), lambda bi, h, qi: (bi, nheads + h)),
            pl.BlockSpec((s, HEAD_W), lambda bi, h, qi: (bi, av0 + h)),
        ]
        args += [kl_arr, p]
    in_specs += [
        pl.BlockSpec((None, 4, DA_HEAD_DIM), lambda bi, h, qi: (layer, 0, 0)),
        pl.BlockSpec((None, 1, HEAD_W), lambda bi, h, qi: (layer, 0, 0)),
    ]
    args += [lam_p, nw.reshape(-1, 1, HEAD_W)]
    vt_rows = HEAD_W + ATT_ONES_ROWS
    scratch = [pltpu.VMEM((1, 2 * tq), _F32), pltpu.VMEM((vt_rows, 2 * tq), _F32), pltpu.VMEM((vt_rows, c), _BF16)]
    tk = min(tk, s)
    if has_lat:
        scratch += [pltpu.VMEM((vt_rows, s), _BF16), pltpu.VMEM((2, tk, 2 * tq), _F32),
                    pltpu.VMEM((2, 1, 2 * tq), _F32)]
    return pl.pallas_call(
        functools.partial(_attn_kernel, has_lat=has_lat, tk=tk, lam_init=lam_init,
                          q_scale=None if has_lat else ATT_Q_SCALE),
        grid=(b, nheads, nq),
        in_specs=in_specs,
        out_specs=pl.BlockSpec((tq, HEAD_W), lambda bi, h, qi: (bi * nq + qi, h)),
        out_shape=jax.ShapeDtypeStruct((b * nq * tq, mix_w), _BF16),
        scratch_shapes=scratch,
        compiler_params=_params(("arbitrary", "arbitrary", "arbitrary")),
        name="diff_attn_lat" if has_lat else "diff_attn_ctx",
    )(*args)


def _branch_kernel(of_ref, ob_ref, hg_ref, cb_ref, cc_ref, cu_ref, ccp_ref, cup_ref, ccn_ref, cun_ref,
                   hnw_ref, cw_ref, a_ref, b_ref, *, n_lat, s, c):
    tm = of_ref.shape[0]
    nw = hnw_ref[...]
    for h in range(of_ref.shape[1] // HEAD_W):
        sl = slice(h * HEAD_W, (h + 1) * HEAD_W)
        o = of_ref[:, sl] + ob_ref[:, sl]
        ms = jnp.mean(o * o, axis=-1, keepdims=True)
        g = hg_ref[:, sl].astype(_F32)
        a_ref[:, sl] = ((o * lax.rsqrt(ms + EPS) * nw) * (g * jax.nn.sigmoid(g))).astype(a_ref.dtype)
    row = pl.program_id(0) * tm + lax.broadcasted_iota(jnp.int32, (tm, 1), 0)
    pos = jnp.where(row < n_lat, row % s, (row - n_lat) % c)
    seq_len = jnp.where(row < n_lat, s, c)
    local = lax.broadcasted_iota(jnp.int32, (tm, 1), 0)
    x = cc_ref[...].astype(_F32) * cu_ref[...].astype(_F32)
    x_halo_p = ccp_ref[7:8, :].astype(_F32) * cup_ref[7:8, :].astype(_F32)
    x_halo_n = ccn_ref[0:1, :].astype(_F32) * cun_ref[0:1, :].astype(_F32)
    x_prev = jnp.where(local == 0, x_halo_p, pltpu.roll(x, 1, axis=0))
    x_prev = jnp.where(pos == 0, 0.0, x_prev)
    x_next = jnp.where(local == tm - 1, x_halo_n, pltpu.roll(x, tm - 1, axis=0))
    x_next = jnp.where(pos == seq_len - 1, 0.0, x_next)
    y = cw_ref[0:1, :] * x_prev + cw_ref[1:2, :] * x + cw_ref[2:3, :] * x_next
    b_ref[...] = (cb_ref[...].astype(_F32) * y).astype(b_ref.dtype)


def _branches(o_fb, p, n_rows, hg_nw, conv_w, layer, b, s, c, mix_w, tm=256):
    tm = math.gcd(math.gcd(tm, s), b * c)
    last8 = p.shape[0] // 8 - 1
    main = lambda col: pl.BlockSpec((tm, mix_w), lambda i: (i, col))
    prev = lambda col: pl.BlockSpec((8, mix_w), lambda i: (jnp.maximum(i * (tm // 8) - 1, 0), col))
    nxt = lambda col: pl.BlockSpec((8, mix_w), lambda i: (jnp.minimum((i + 1) * (tm // 8), last8), col))
    out_sds = jax.ShapeDtypeStruct((n_rows, mix_w), _BF16)
    return pl.pallas_call(
        functools.partial(_branch_kernel, n_lat=b * s, s=s, c=c),
        grid=(n_rows // tm,),
        in_specs=[
            pl.BlockSpec((None, tm, mix_w), lambda i: (0, i, 0)),
            pl.BlockSpec((None, tm, mix_w), lambda i: (1, i, 0)),
            main(4), main(5), main(6), main(7), prev(6), prev(7), nxt(6), nxt(7),
            pl.BlockSpec((None, 1, HEAD_W), lambda i: (layer, 0, 0)),
            pl.BlockSpec((None, 3, mix_w), lambda i: (layer, 0, 0)),
        ],
        out_specs=[pl.BlockSpec((tm, mix_w), lambda i: (i, 0)), pl.BlockSpec((tm, mix_w), lambda i: (i, 0))],
        out_shape=[out_sds, out_sds],
        compiler_params=_params(("parallel",)),
        name="branch_prep",
    )(o_fb, o_fb, p, p, p, p, p, p, p, p, hg_nw.reshape(-1, 1, HEAD_W), conv_w)


def _merge_kernel(h_ref, mod_ref, a_ref, b_ref, c_ref, ga_ref, gb_ref, gc_ref, wa_ref, wb_ref, wc_ref,
                  wo_ref, o_ref):
    j = pl.program_id(1)

    @pl.when(j == 0)
    def _():
        o_ref[...] = jnp.zeros_like(o_ref)

    def gated(g_ref, x_ref, w_ref):
        return jax.nn.sigmoid(g_ref[...].astype(_F32)) * _dot(x_ref[...], w_ref[...])

    y = gated(ga_ref, a_ref, wa_ref) + gated(gb_ref, b_ref, wb_ref) + gated(gc_ref, c_ref, wc_ref)
    o_ref[...] += _dot(y.astype(_BF16), wo_ref[...])

    @pl.when(j == pl.num_programs(1) - 1)
    def _():
        o_ref[...] = h_ref[...] + mod_ref[5:6, :] * o_ref[...]


def _merge(h, n_rows, mod_l, a, bb, cc, p, w_branch, w_out, layer, dims, mix_w, tn=512):
    d = h.shape[1]
    tm = dims["tm"]
    mod_row = dims["mod_row"]
    g0 = 11 * mix_w // tn
    gstep = d // tn
    br = lambda: pl.BlockSpec((tm, mix_w), lambda i, j: (i, 0))
    gate = lambda k: pl.BlockSpec((tm, tn), lambda i, j: (i, g0 + k * gstep + j))
    wbr = lambda k: pl.BlockSpec((None, None, mix_w, tn), lambda i, j: (layer, k, 0, j))
    return pl.pallas_call(
        _merge_kernel,
        grid=(pl.cdiv(n_rows, tm), d // tn),
        in_specs=[
            pl.BlockSpec((tm, d), lambda i, j: (i, 0)),
            pl.BlockSpec((None, N_MOD, d), lambda i, j: (mod_row(i), 0, 0)),
            br(), br(), br(), gate(0), gate(1), gate(2), wbr(0), wbr(1), wbr(2),
            pl.BlockSpec((None, tn, d), lambda i, j: (layer, j, 0)),
        ],
        out_specs=pl.BlockSpec((tm, d), lambda i, j: (i, 0)),
        out_shape=jax.ShapeDtypeStruct((n_rows, d), _F32),
        compiler_params=_params(("parallel", "arbitrary")),
        name="merge_out",
    )(h, mod_l, a, bb, cc, p, p, p, w_branch, w_branch, w_branch, w_out)


def kernel(x, c, ctx, c_ctx, w_ada, b_ada, norm_w, ffn_w13, ffn_w2, w_in, hgrn_lb, hgrn_norm_w, conv_w,
           da_lambda, da_norm_w, w_branch, w_out, final_norm_w):
    b, s, d = x.shape
    cl = ctx.shape[1]
    depth = w_ada.shape[0]
    mix_w = d // 2
    n_lat, n_ctx = b * s, b * cl
    n_all = n_lat + n_ctx
    tm = math.gcd(math.gcd(512, s), n_ctx)
    dims = {
        "tm": tm,
        "tm_big": math.gcd(1024, s),
        "s": s,
        "b": b,
        "mod_row": lambda i: jnp.minimum(i * tm // s, b),
        "norm_w": norm_w.reshape(depth * 3, 1, d),
    }

    w13_b, w2_b, w_in_b = ffn_w13.astype(_BF16), ffn_w2.astype(_BF16), w_in.astype(_BF16)
    w_branch_b, w_out_b = w_branch.astype(_BF16), w_out.astype(_BF16)

    cvec = jnp.concatenate([c, c_ctx[None, :], jnp.zeros((8 - b - 1, d), _F32)], axis=0)
    mod = _ada_mod(cvec, w_ada, b_ada).reshape(depth, 8, N_MOD, d)

    cum = jnp.cumsum(jax.nn.softmax(hgrn_lb.astype(_F32), axis=1), axis=1)
    lb = cum - cum[:, :1]
    lb_tab = jnp.stack([jnp.log(lb), jnp.log1p(-lb), 1.0 - lb], axis=2)
    w_tab, lmap = _hgrn_tables()
    cos, sin = _rope_tables(s)

    h = jnp.concatenate([x.reshape(n_lat, d), ctx.reshape(n_ctx, d)], axis=0)
    for layer in range(depth):
        last = layer == depth - 1
        lam_init = 0.8 - 0.6 * math.exp(-0.3 * layer)
        mod_l = mod[layer]
        h = _ffn(h, n_all, mod_l, layer * 3, w13_b, w2_b, layer, 0, 0, dims)
        p = _mixer_in(h, mod_l, layer * 3 + 1, w_in_b, layer, dims)
        qk = _rope(p, cos, sin, b, s, 8 * mix_w)
        o_fb = _hgrn(p, lb_tab[:, layer], w_tab, lmap, b, s, cl, mix_w)
        tq = min(512, s)
        c_att = _attn(qk, 0, 0, tq, s // tq, p, qk, b, s, cl, mix_w, da_lambda, da_norm_w, layer, lam_init, True)
        n_rows = n_lat if last else n_all
        if not last:
            c_ctx_att = _attn(p, 8 * mix_w, n_lat, cl, 1, p, None, b, s, cl, mix_w, da_lambda, da_norm_w,
                              layer, lam_init, False)
            c_att = jnp.concatenate([c_att, c_ctx_att], axis=0)
        a_br, b_br = _branches(o_fb, p, n_rows, hgrn_norm_w, conv_w, layer, b, s, cl, mix_w)
        h = _merge(h, n_rows, mod_l, a_br, b_br, c_att, p, w_branch_b, w_out_b, layer, dims, mix_w)
        h = _ffn(h, n_rows, mod_l, layer * 3 + 2, w13_b, w2_b, layer, 1, 6, dims,
                 final_w=final_norm_w if last else None)
    return h.reshape(b, s, d)
```

```python
import functools
import math

import jax
import jax.numpy as jnp
import numpy as np
from jax import lax
from jax.experimental import pallas as pl
from jax.experimental.pallas import tpu as pltpu

_F32 = jnp.float32
_BF16 = jnp.bfloat16

EPS = 1e-6
N_MOD = 9
GRID_W = 64
ROPE_THETA = 10000.0
HEAD_W = 128
DA_HEAD_DIM = 64
LOG2_E = math.log2(math.e)
ATT_Q_SCALE = LOG2_E * DA_HEAD_DIM ** -0.5
ATT_Q_GROUP = 1024
ATT_ONES_ROWS = 16
HG_CHUNK = 64
HG_LEVELS = 6
HG_HEAD_GROUP = 8
ROW_CHUNK = 128
V7X_VMEM_LIMIT = 56 * 1024 * 1024
V7X_VMEM_LIMIT_FFN = 60 * 1024 * 1024


def _dot(a, b):
    return jnp.dot(a, b, preferred_element_type=_F32)


def _dot_nt(a, b):
    return lax.dot_general(a, b, (((1,), (1,)), ((), ())), preferred_element_type=_F32)


def _dot_tn(a, b):
    return lax.dot_general(a, b, (((0,), (0,)), ((), ())), preferred_element_type=_F32)


def _params(semantics, vmem=V7X_VMEM_LIMIT):
    return pltpu.CompilerParams(dimension_semantics=semantics, vmem_limit_bytes=vmem)


def _for_row_chunks(n_rows, body):
    def step(c, carry):
        body(pl.multiple_of(c * ROW_CHUNK, ROW_CHUNK))
        return carry
    lax.fori_loop(0, n_rows // ROW_CHUNK, step, 0)


def _norm_mod(h, nw, shift, scale):
    ms = jnp.mean(h * h, axis=-1, keepdims=True)
    return (h * lax.rsqrt(ms + EPS) * nw) * (1.0 + scale) + shift


def _ada_kernel(c_ref, w_ref, b_ref, o_ref):
    c = c_ref[...]
    s = (c * jax.nn.sigmoid(c)).astype(_BF16)
    o_ref[...] = _dot(s, w_ref[...].astype(_BF16)) + b_ref[...]


def _ada_mod(cvec, w_ada, b_ada, tn=1024):
    depth, d, nd = w_ada.shape
    rows = cvec.shape[0]
    return pl.pallas_call(
        _ada_kernel,
        grid=(depth, nd // tn),
        in_specs=[
            pl.BlockSpec((rows, d), lambda l, j: (0, 0)),
            pl.BlockSpec((None, d, tn), lambda l, j: (l, 0, j)),
            pl.BlockSpec((None, 1, tn), lambda l, j: (l, 0, j)),
        ],
        out_specs=pl.BlockSpec((None, rows, tn), lambda l, j: (l, 0, j)),
        out_shape=jax.ShapeDtypeStruct((depth, rows, nd), _F32),
        compiler_params=_params(("arbitrary", "arbitrary")),
        name="ada_mod",
    )(cvec, w_ada, b_ada.reshape(depth, 1, nd))


def _on_row_tiles(tail_rows, tm, body):
    if tail_rows == 0:
        body(tm)
        return
    is_tail = pl.program_id(0) == pl.num_programs(0) - 1

    @pl.when(jnp.logical_not(is_tail))
    def _():
        body(tm)

    @pl.when(is_tail)
    def _():
        body(tail_rows)


def _ffn_kernel(h_ref, mod_ref, nw_ref, w1_ref, w3_ref, w2_ref, *rest, mod_off, final_norm, tail_rows):
    if final_norm:
        fw_ref, o_ref, xn_ref = rest
    else:
        o_ref, xn_ref = rest
    j = pl.program_id(1)

    def body(n_rows):
        @pl.when(j == 0)
        def _():
            def rows(r0):
                rs = pl.ds(r0, ROW_CHUNK)
                xn = _norm_mod(h_ref[rs, :], nw_ref[...], mod_ref[mod_off:mod_off + 1, :],
                               mod_ref[mod_off + 1:mod_off + 2, :])
                xn_ref[rs, :] = xn.astype(_BF16)
                o_ref[rs, :] = jnp.zeros((ROW_CHUNK, o_ref.shape[1]), o_ref.dtype)
            _for_row_chunks(n_rows, rows)

        xn = xn_ref[0:n_rows, :]
        g = _dot(xn, w1_ref[...])
        u = _dot(xn, w3_ref[...])
        a = (g * jax.nn.sigmoid(g) * u).astype(_BF16)
        o_ref[0:n_rows, :] += _dot(a, w2_ref[...])

        @pl.when(j == pl.num_programs(1) - 1)
        def _():
            gate = mod_ref[mod_off + 2:mod_off + 3, :]

            def rows(r0):
                rs = pl.ds(r0, ROW_CHUNK)
                out = h_ref[rs, :] + 0.5 * gate * o_ref[rs, :]
                if final_norm:
                    ms = jnp.mean(out * out, axis=-1, keepdims=True)
                    out = out * lax.rsqrt(ms + EPS) * fw_ref[...]
                o_ref[rs, :] = out
            _for_row_chunks(n_rows, rows)

    _on_row_tiles(tail_rows, h_ref.shape[0], body)


def _ffn(h, n_rows, mod_l, nw, w13, w2, layer, which, mod_off, dims, final_w=None, tf=512):
    d = h.shape[1]
    f = w2.shape[2]
    tm = dims["tm_big"]
    nfb = f // tf
    mod_row = lambda i: jnp.minimum(i * tm // dims["s"], dims["b"])
    in_specs = [
        pl.BlockSpec((tm, d), lambda i, j: (i, 0)),
        pl.BlockSpec((None, N_MOD, d), lambda i, j: (mod_row(i), 0, 0)),
        pl.BlockSpec((None, 1, d), lambda i, j: (nw, 0, 0)),
        pl.BlockSpec((None, None, d, tf), lambda i, j: (layer, which, 0, j)),
        pl.BlockSpec((None, None, d, tf), lambda i, j: (layer, which, 0, j + nfb)),
        pl.BlockSpec((None, None, tf, d), lambda i, j: (layer, which, j, 0)),
    ]
    args = [h, mod_l, dims["norm_w"], w13, w13, w2]
    if final_w is not None:
        in_specs.append(pl.BlockSpec((1, d), lambda i, j: (0, 0)))
        args.append(final_w.reshape(1, d))
    return pl.pallas_call(
        functools.partial(_ffn_kernel, mod_off=mod_off, final_norm=final_w is not None, tail_rows=n_rows % tm),
        grid=(pl.cdiv(n_rows, tm), nfb),
        in_specs=in_specs,
        out_specs=pl.BlockSpec((tm, d), lambda i, j: (i, 0)),
        out_shape=jax.ShapeDtypeStruct((n_rows, d), _F32),
        scratch_shapes=[pltpu.VMEM((tm, d), _BF16)],
        compiler_params=_params(("parallel", "arbitrary"), vmem=V7X_VMEM_LIMIT_FFN),
        name="half_ffn",
    )(*args)


def _proj_kernel(h_ref, mod_ref, nw_ref, w_ref, o_ref, xn_ref, *, tail_rows):
    def body(n_rows):
        @pl.when(pl.program_id(1) == 0)
        def _():
            xn = _norm_mod(h_ref[0:n_rows, :], nw_ref[...], mod_ref[3:4, :], mod_ref[4:5, :])
            xn_ref[0:n_rows, :] = xn.astype(_BF16)

        o_ref[0:n_rows, :] = _dot(xn_ref[0:n_rows, :], w_ref[...]).astype(o_ref.dtype)

    _on_row_tiles(tail_rows, h_ref.shape[0], body)


def _mixer_in(h, mod_l, nw, w_in, layer, dims, tn=1024):
    n, d = h.shape
    cols = w_in.shape[2]
    tm = dims["tm_big"]
    mod_row = lambda i: jnp.minimum(i * tm // dims["s"], dims["b"])
    return pl.pallas_call(
        functools.partial(_proj_kernel, tail_rows=n % tm),
        grid=(pl.cdiv(n, tm), cols // tn),
        in_specs=[
            pl.BlockSpec((tm, d), lambda i, j: (i, 0)),
            pl.BlockSpec((None, N_MOD, d), lambda i, j: (mod_row(i), 0, 0)),
            pl.BlockSpec((None, 1, d), lambda i, j: (nw, 0, 0)),
            pl.BlockSpec((None, d, tn), lambda i, j: (layer, 0, j)),
        ],
        out_specs=pl.BlockSpec((tm, tn), lambda i, j: (i, j)),
        out_shape=jax.ShapeDtypeStruct((n, cols), _BF16),
        scratch_shapes=[pltpu.VMEM((tm, d), _BF16)],
        compiler_params=_params(("parallel", "arbitrary")),
        name="mixer_in",
    )(h, mod_l, dims["norm_w"], w_in)


def _rope_kernel(x_ref, cos_ref, sin_ref, o_ref):
    cos = cos_ref[...]
    sin = sin_ref[...]
    lane = lax.broadcasted_iota(jnp.int32, cos.shape, 1)
    first = (lane % 32) < 16
    n_groups = x_ref.shape[1] // HEAD_W
    for g in range(n_groups):
        sl = slice(g * HEAD_W, (g + 1) * HEAD_W)
        x = x_ref[:, sl].astype(_F32)
        rot = jnp.where(first, pltpu.roll(x, HEAD_W - 16, axis=1), pltpu.roll(x, 16, axis=1))
        y = x * cos + rot * sin
        if g < n_groups // 2:
            y = y * ATT_Q_SCALE
        o_ref[:, sl] = y.astype(o_ref.dtype)


def _rope_tables(n_tokens):
    rows = n_tokens // GRID_W
    row = jnp.repeat(jnp.arange(rows, dtype=jnp.int32), GRID_W)
    col = jnp.tile(jnp.arange(GRID_W, dtype=jnp.int32), rows)
    half = DA_HEAD_DIM // 2
    inv_freq = ROPE_THETA ** (-jnp.arange(0, half, 2, dtype=_F32) / half)
    ang_r = row.astype(_F32)[:, None] * inv_freq
    ang_c = col.astype(_F32)[:, None] * inv_freq
    ang = jnp.concatenate([ang_r, ang_r, ang_c, ang_c], axis=-1)
    ang = jnp.concatenate([ang, ang], axis=-1)
    sign = jnp.where((jnp.arange(HEAD_W) % 32) < 16, -1.0, 1.0)
    return jnp.cos(ang), jnp.sin(ang) * sign


def _rope(p, cos, sin, b, s, col0, tr=512):
    width = 2 * dims_mix_w(p)
    tr = min(tr, s)
    return pl.pallas_call(
        _rope_kernel,
        grid=(b * s // tr,),
        in_specs=[
            pl.BlockSpec((tr, width), lambda i: (i, col0 // width)),
            pl.BlockSpec((tr, HEAD_W), lambda i: (i % (s // tr), 0)),
            pl.BlockSpec((tr, HEAD_W), lambda i: (i % (s // tr), 0)),
        ],
        out_specs=pl.BlockSpec((tr, width), lambda i: (i, 0)),
        out_shape=jax.ShapeDtypeStruct((b * s, width), _BF16),
        compiler_params=_params(("parallel",)),
        name="axial_rope",
    )(p, cos, sin)


def dims_mix_w(p):
    return p.shape[1] // 17


def _hgrn_tables():
    L = HG_CHUNK
    w = np.zeros((2, (HG_LEVELS + 2) * L + 8, L), np.float32)
    lmap = np.full((2, L, L), HG_LEVELS + 1, np.int32)
    r = np.arange(L)
    for t in range(L):
        w[0, t, r <= t] = 1
        w[0, L + t, r > t] = 1
        w[1, t, r >= t] = 1
        w[1, L + t, r < t] = 1
        lmap[:, t, t] = 0
        for lvl in range(1, HG_LEVELS + 1):
            bs = (2 * L) >> lvl
            mid = (t // bs) * bs + bs // 2
            base = (lvl + 1) * L + t
            if t >= mid:
                w[0, base, (r >= mid) & (r <= t)] = 1
                w[1, base, (r >= mid) & (r < t)] = 1
            else:
                w[0, base, (r > t) & (r < mid)] = 1
                w[1, base, (r >= t) & (r < mid)] = 1
            for s_ in range(L):
                if s_ // bs == t // bs and t >= mid and s_ < mid:
                    lmap[0, t, s_] = lvl
                    lmap[1, s_, t] = lvl
    w[:, (HG_LEVELS + 2) * L:, :] = 1
    w = np.concatenate([w, w], axis=2)
    return jnp.asarray(w, _BF16), jnp.asarray(lmap)


def _hgrn_kernel(q_ref, z_ref, v_ref, lb_ref, w_ref, lmap_ref, o_ref, st_ref):
    L = HG_CHUNK

    @pl.when(pl.program_id(2) == 0)
    def _():
        st_ref[...] = jnp.zeros_like(st_ref)

    n_heads = q_ref.shape[1] // HEAD_W
    lmap = lmap_ref[...]
    for h0 in range(0, n_heads, HG_HEAD_GROUP):
        gs = slice(h0 * HEAD_W, (h0 + HG_HEAD_GROUP) * HEAD_W)
        heads = [slice(h * HEAD_W, (h + 1) * HEAD_W) for h in range(HG_HEAD_GROUP)]
        z = z_ref[:, gs].astype(_F32)
        ls = jnp.minimum(z, 0.0) - jnp.log(1.0 + jnp.exp(-jnp.abs(z)))
        bb = lb_ref[1:2, gs] + ls
        log_lb = lb_ref[0:1, gs]
        lf = jnp.maximum(log_lb, bb) + jnp.log(1.0 + jnp.exp(-jnp.abs(log_lb - bb)))
        k = lb_ref[2:3, gs] * jnp.exp(ls - z)
        lf2 = lf * LOG2_E
        lf_hi = lf2.astype(_BF16)
        lf_lo = (lf2 - lf_hi.astype(_F32)).astype(_BF16)
        e = jnp.exp2(_dot(w_ref[...], jnp.concatenate([lf_hi, lf_lo], axis=0)))
        q = q_ref[:, gs].astype(_F32)
        v = v_ref[:, gs]
        qk = q * k
        a = [jnp.where(lmap == 0, jnp.sum(qk[:, sl], axis=-1, keepdims=True) * jnp.ones((1, L), _F32), 0.0)
             for sl in heads]
        for lvl in range(1, HG_LEVELS + 1):
            el = e[(lvl + 1) * L:(lvl + 2) * L]
            ql = (q * el).astype(_BF16)
            kl = (k * el).astype(_BF16)
            a = [jnp.where(lmap == lvl, _dot_nt(ql[:, sl], kl[:, sl]), a[h]) for h, sl in enumerate(heads)]
        q_in = (q * e[0:L]).astype(_BF16)
        k_out = (k * e[L:2 * L]).astype(_BF16)
        tot = e[(HG_LEVELS + 2) * L:(HG_LEVELS + 2) * L + 1]
        for h, sl in enumerate(heads):
            st = st_ref[h0 + h]
            osl = slice((h0 + h) * HEAD_W, (h0 + h + 1) * HEAD_W)
            o_ref[:, osl] = _dot(a[h].astype(_BF16), v[:, sl]) + _dot_nt(q_in[:, sl], st.astype(_BF16))
            st_ref[h0 + h] = tot[:, sl] * st + _dot_tn(v[:, sl], k_out[:, sl])


def _hgrn(p, lb_tab, w_tab, lmap, b, s, c, mix_w):
    n = p.shape[0]
    L = HG_CHUNK
    ncc, ncl = c // L, s // L
    nheads = mix_w // HEAD_W

    def row_block(bi, di, j):
        is_ctx = j < ncc
        cj = jnp.where(di == 0, j, ncc - 1 - j)
        lj = jnp.where(di == 0, j - ncc, ncl - 1 - (j - ncc))
        return jnp.where(is_ctx, (b * s + bi * c) // L + cj, bi * ncl + lj)

    def spec(col):
        return pl.BlockSpec((L, mix_w), lambda bi, di, j: (row_block(bi, di, j), col))

    return pl.pallas_call(
        _hgrn_kernel,
        grid=(b, 2, ncc + ncl),
        in_specs=[
            spec(0),
            pl.BlockSpec((L, mix_w), lambda bi, di, j: (row_block(bi, di, j), 1 + di)),
            spec(3),
            pl.BlockSpec((None, 3, mix_w), lambda bi, di, j: (di, 0, 0)),
            pl.BlockSpec((None,) + w_tab.shape[1:], lambda bi, di, j: (di, 0, 0)),
            pl.BlockSpec((None, L, L), lambda bi, di, j: (di, 0, 0)),
        ],
        out_specs=pl.BlockSpec((None, L, mix_w), lambda bi, di, j: (di, row_block(bi, di, j), 0)),
        out_shape=jax.ShapeDtypeStruct((2, n, mix_w), _F32),
        scratch_shapes=[pltpu.VMEM((nheads, HEAD_W, HEAD_W), _F32)],
        compiler_params=_params(("arbitrary", "arbitrary", "arbitrary")),
        name="hgrn_scan",
    )(p, p, p, lb_tab, w_tab, lmap)


def _attn_kernel(*refs, has_lat, tk, lam_init, q_scale):
    if has_lat:
        (q_ref, kc_ref, vc_ref, kl_ref, vl_ref, lamp_ref, nw_ref, o_ref, m_ref, acc_ref, vtc_ref, vtl_ref,
         st_ref, mx_ref) = refs
    else:
        q_ref, kc_ref, vc_ref, lamp_ref, nw_ref, o_ref, m_ref, acc_ref, vtc_ref = refs
    tq = q_ref.shape[0]
    eye = (lax.broadcasted_iota(jnp.int32, (HEAD_W, HEAD_W), 0)
           == lax.broadcasted_iota(jnp.int32, (HEAD_W, HEAD_W), 1)).astype(_F32).astype(_BF16)

    @pl.when(pl.program_id(2) == 0)
    def _():
        ones = jnp.ones((ATT_ONES_ROWS, vtc_ref.shape[1]), _BF16)
        vtc_ref[0:HEAD_W, :] = _dot_nt(eye, vc_ref[...]).astype(_BF16)
        vtc_ref[HEAD_W:, :] = ones
        if has_lat:
            def tbody(i, carry):
                off = pl.multiple_of(i * tk, tk)
                vtl_ref[0:HEAD_W, pl.ds(off, tk)] = _dot_nt(eye, vl_ref[pl.ds(off, tk), :]).astype(_BF16)
                vtl_ref[HEAD_W:, pl.ds(off, tk)] = jnp.ones((ATT_ONES_ROWS, tk), _BF16)
                return carry
            lax.fori_loop(0, vl_ref.shape[0] // tk, tbody, 0)

    q = q_ref[...].astype(_F32)
    if q_scale is not None:
        q = q * q_scale
    lane = lax.broadcasted_iota(jnp.int32, q.shape, 1)
    qq = jnp.concatenate([jnp.where(lane < DA_HEAD_DIM, q, 0.0), jnp.where(lane >= DA_HEAD_DIM, q, 0.0)],
                         axis=0).astype(_BF16)

    m_ref[...] = jnp.full_like(m_ref, -jnp.inf)
    acc_ref[...] = jnp.zeros_like(acc_ref)

    def softmax_pv(st, st_max, vt):
        group = min(ATT_Q_GROUP, 2 * tq)
        for c0 in range(0, 2 * tq, group):
            cs = slice(c0, c0 + group)
            m_prev = m_ref[:, cs]
            m_new = jnp.maximum(m_prev, st_max[:, cs])
            alpha = jnp.exp2(m_prev - m_new)
            pt = jnp.exp2(st[:, cs] - m_new).astype(_BF16)
            acc_ref[:, cs] = alpha * acc_ref[:, cs] + _dot(vt, pt)
            m_ref[:, cs] = m_new

    def scores(t, buf):
        off = pl.multiple_of((t - 1) * tk, tk)
        st = _dot_nt(kl_ref[pl.ds(off, tk), :], qq)
        st_ref[buf] = st
        mx_ref[buf] = jnp.max(st, axis=0, keepdims=True)

    st_c = _dot_nt(kc_ref[...], qq)
    mx_c = jnp.max(st_c, axis=0, keepdims=True)
    if not has_lat:
        softmax_pv(st_c, mx_c, vtc_ref[...])
    else:
        n = kl_ref.shape[0] // tk

        def consume(t, buf):
            off = pl.multiple_of((t - 1) * tk, tk)
            softmax_pv(st_ref[buf], mx_ref[buf], vtl_ref[:, pl.ds(off, tk)])

        scores(1, 1)
        softmax_pv(st_c, mx_c, vtc_ref[...])
        n_pairs = (n - 1) // 2

        def body(j, carry):
            t = 2 * j + 1
            scores(t + 1, 0)
            consume(t, 1)
            scores(t + 2, 1)
            consume(t + 1, 0)
            return carry
        lax.fori_loop(0, n_pairs, body, 0)
        for t in range(2 * n_pairs + 1, n + 1):
            if t < n:
                scores(t + 1, (t + 1) % 2)
            consume(t, t % 2)

    lp = lamp_ref[...]
    lam = (jnp.exp(jnp.sum(lp[0:1] * lp[1:2], axis=-1, keepdims=True))
           - jnp.exp(jnp.sum(lp[2:3] * lp[3:4], axis=-1, keepdims=True)) + lam_init)
    on = acc_ref[0:HEAD_W, :] / acc_ref[HEAD_W:HEAD_W + 1, :]
    ot = on[:, 0:tq] - lam * on[:, tq:2 * tq]
    ms = jnp.mean(ot * ot, axis=0, keepdims=True)
    o = (ot * lax.rsqrt(ms + EPS)).T
    o_ref[...] = (o * nw_ref[...] * (1.0 - lam_init)).astype(o_ref.dtype)


def _attn(q_arr, q_col0, q_row0, tq, nq, p, kl_arr, b, s, c, mix_w, lam_p, nw, layer, lam_init, has_lat, tk=1024):
    nheads = mix_w // HEAD_W
    ak0, av0 = 9 * mix_w // HEAD_W, 10 * mix_w // HEAD_W
    ctx_blk0 = b * s // c
    in_specs = [
        pl.BlockSpec((tq, HEAD_W), lambda bi, h, qi: (q_row0 // tq + bi * nq + qi, q_col0 // HEAD_W + h)),
        pl.BlockSpec((c, HEAD_W), lambda bi, h, qi: (ctx_blk0 + bi, ak0 + h)),
        pl.BlockSpec((c, HEAD_W), lambda bi, h, qi: (ctx_blk0 + bi, av0 + h)),
    ]
    args = [q_arr, p, p]
    if has_lat:
        in_specs += [
            pl.BlockSpec((s, HEAD_W), lambda bi, h, qi: (bi, nheads + h)),
            pl.BlockSpec((s, HEAD_W), lambda bi, h, qi: (bi, av0 + h)),
        ]
        args += [kl_arr, p]
    in_specs += [
        pl.BlockSpec((None, 4, DA_HEAD_DIM), lambda bi, h, qi: (layer, 0, 0)),
        pl.BlockSpec((None, 1, HEAD_W), lambda bi, h, qi: (layer, 0, 0)),
    ]
    args += [lam_p, nw.reshape(-1, 1, HEAD_W)]
    vt_rows = HEAD_W + ATT_ONES_ROWS
    scratch = [pltpu.VMEM((1, 2 * tq), _F32), pltpu.VMEM((vt_rows, 2 * tq), _F32), pltpu.VMEM((vt_rows, c), _BF16)]
    tk = min(tk, s)
    if has_lat:
        scratch += [pltpu.VMEM((vt_rows, s), _BF16), pltpu.VMEM((2, tk, 2 * tq), _F32),
                    pltpu.VMEM((2, 1, 2 * tq), _F32)]
    return pl.pallas_call(
        functools.partial(_attn_kernel, has_lat=has_lat, tk=tk, lam_init=lam_init,
                          q_scale=None if has_lat else ATT_Q_SCALE),
        grid=(b, nheads, nq),
        in_specs=in_specs,
        out_specs=pl.BlockSpec((tq, HEAD_W), lambda bi, h, qi: (bi * nq + qi, h)),
        out_shape=jax.ShapeDtypeStruct((b * nq * tq, mix_w), _BF16),
        scratch_shapes=scratch,
        compiler_params=_params(("arbitrary", "arbitrary", "arbitrary")),
        name="diff_attn_lat" if has_lat else "diff_attn_ctx",
    )(*args)


def _branch_kernel(of_ref, ob_ref, hg_ref, cb_ref, cc_ref, cu_ref, ccp_ref, cup_ref, ccn_ref, cun_ref,
                   hnw_ref, cw_ref, a_ref, b_ref, *, n_lat, s, c):
    tm = of_ref.shape[0]
    nw = hnw_ref[...]
    for h in range(of_ref.shape[1] // HEAD_W):
        sl = slice(h * HEAD_W, (h + 1) * HEAD_W)
        o = of_ref[:, sl] + ob_ref[:, sl]
        ms = jnp.mean(o * o, axis=-1, keepdims=True)
        g = hg_ref[:, sl].astype(_F32)
        a_ref[:, sl] = ((o * lax.rsqrt(ms + EPS) * nw) * (g * jax.nn.sigmoid(g))).astype(a_ref.dtype)
    row = pl.program_id(0) * tm + lax.broadcasted_iota(jnp.int32, (tm, 1), 0)
    pos = jnp.where(row < n_lat, row % s, (row - n_lat) % c)
    seq_len = jnp.where(row < n_lat, s, c)
    local = lax.broadcasted_iota(jnp.int32, (tm, 1), 0)
    x = cc_ref[...].astype(_F32) * cu_ref[...].astype(_F32)
    x_halo_p = ccp_ref[7:8, :].astype(_F32) * cup_ref[7:8, :].astype(_F32)
    x_halo_n = ccn_ref[0:1, :].astype(_F32) * cun_ref[0:1, :].astype(_F32)
    x_prev = jnp.where(local == 0, x_halo_p, pltpu.roll(x, 1, axis=0))
    x_prev = jnp.where(pos == 0, 0.0, x_prev)
    x_next = jnp.where(local == tm - 1, x_halo_n, pltpu.roll(x, tm - 1, axis=0))
    x_next = jnp.where(pos == seq_len - 1, 0.0, x_next)
    y = cw_ref[0:1, :] * x_prev + cw_ref[1:2, :] * x + cw_ref[2:3, :] * x_next
    b_ref[...] = (cb_ref[...].astype(_F32) * y).astype(b_ref.dtype)


def _branches(o_fb, p, n_rows, hg_nw, conv_w, layer, b, s, c, mix_w, tm=256):
    tm = math.gcd(math.gcd(tm, s), b * c)
    last8 = p.shape[0] // 8 - 1
    main = lambda col: pl.BlockSpec((tm, mix_w), lambda i: (i, col))
    prev = lambda col: pl.BlockSpec((8, mix_w), lambda i: (jnp.maximum(i * (tm // 8) - 1, 0), col))
    nxt = lambda col: pl.BlockSpec((8, mix_w), lambda i: (jnp.minimum((i + 1) * (tm // 8), last8), col))
    out_sds = jax.ShapeDtypeStruct((n_rows, mix_w), _BF16)
    return pl.pallas_call(
        functools.partial(_branch_kernel, n_lat=b * s, s=s, c=c),
        grid=(n_rows // tm,),
        in_specs=[
            pl.BlockSpec((None, tm, mix_w), lambda i: (0, i, 0)),
            pl.BlockSpec((None, tm, mix_w), lambda i: (1, i, 0)),
            main(4), main(5), main(6), main(7), prev(6), prev(7), nxt(6), nxt(7),
            pl.BlockSpec((None, 1, HEAD_W), lambda i: (layer, 0, 0)),
            pl.BlockSpec((None, 3, mix_w), lambda i: (layer, 0, 0)),
        ],
        out_specs=[pl.BlockSpec((tm, mix_w), lambda i: (i, 0)), pl.BlockSpec((tm, mix_w), lambda i: (i, 0))],
        out_shape=[out_sds, out_sds],
        compiler_params=_params(("parallel",)),
        name="branch_prep",
    )(o_fb, o_fb, p, p, p, p, p, p, p, p, hg_nw.reshape(-1, 1, HEAD_W), conv_w)


def _merge_kernel(h_ref, mod_ref, a_ref, b_ref, c_ref, cx_ref, ga_ref, gb_ref, gc_ref, wa_ref, wb_ref, wc_ref,
                  wo_ref, o_ref, *, n_lat_tiles):
    j = pl.program_id(1)
    tn = ga_ref.shape[1]

    @pl.when(j == 0)
    def _():
        o_ref[...] = jnp.zeros_like(o_ref)

    def gated(g_ref, x, w_ref):
        return jax.nn.sigmoid(g_ref[...].astype(_F32)) * _dot(x, w_ref[...])

    c = jnp.where(pl.program_id(0) >= n_lat_tiles, cx_ref[...], c_ref[...])
    y = gated(ga_ref, a_ref[...], wa_ref) + gated(gb_ref, b_ref[...], wb_ref) + gated(gc_ref, c, wc_ref)
    o_ref[...] += _dot(y.astype(_BF16), wo_ref[pl.ds(pl.multiple_of(j * tn, tn), tn), :])

    @pl.when(j == pl.num_programs(1) - 1)
    def _():
        o_ref[...] = h_ref[...] + mod_ref[5:6, :] * o_ref[...]


def _merge(h, n_rows, mod_l, a, bb, c_lat, c_ctx, p, w_branch, w_out, layer, dims, mix_w, tn=512):
    d = h.shape[1]
    tm = dims["tm"]
    mod_row = dims["mod_row"]
    g0 = 11 * mix_w // tn
    gstep = d // tn
    n_lat_tiles = c_lat.shape[0] // tm
    if c_ctx is None:
        c_ctx = c_lat
    br = lambda: pl.BlockSpec((tm, mix_w), lambda i, j: (i, 0))
    gate = lambda k: pl.BlockSpec((tm, tn), lambda i, j: (i, g0 + k * gstep + j))
    wbr = lambda k: pl.BlockSpec((None, None, mix_w, tn), lambda i, j: (layer, k, 0, j))
    return pl.pallas_call(
        functools.partial(_merge_kernel, n_lat_tiles=n_lat_tiles),
        grid=(n_rows // tm, d // tn),
        in_specs=[
            pl.BlockSpec((tm, d), lambda i, j: (i, 0)),
            pl.BlockSpec((None, N_MOD, d), lambda i, j: (mod_row(i), 0, 0)),
            br(), br(),
            pl.BlockSpec((tm, mix_w), lambda i, j: (jnp.minimum(i, n_lat_tiles - 1), 0)),
            pl.BlockSpec((tm, mix_w), lambda i, j: (jnp.maximum(i - n_lat_tiles, 0), 0)),
            gate(0), gate(1), gate(2), wbr(0), wbr(1), wbr(2),
            pl.BlockSpec((None, d, d), lambda i, j: (layer, 0, 0), pipeline_mode=pl.Buffered(1)),
        ],
        out_specs=pl.BlockSpec((tm, d), lambda i, j: (i, 0)),
        out_shape=jax.ShapeDtypeStruct((n_rows, d), _F32),
        compiler_params=_params(("parallel", "arbitrary")),
        name="merge_out",
    )(h, mod_l, a, bb, c_lat, c_ctx, p, p, p, w_branch, w_branch, w_branch, w_out)


def kernel(x, c, ctx, c_ctx, w_ada, b_ada, norm_w, ffn_w13, ffn_w2, w_in, hgrn_lb, hgrn_norm_w, conv_w,
           da_lambda, da_norm_w, w_branch, w_out, final_norm_w):
    b, s, d = x.shape
    cl = ctx.shape[1]
    depth = w_ada.shape[0]
    mix_w = d // 2
    n_lat, n_ctx = b * s, b * cl
    n_all = n_lat + n_ctx
    tm = math.gcd(math.gcd(512, s), n_ctx)
    dims = {
        "tm": tm,
        "tm_big": math.gcd(1024, s),
        "s": s,
        "b": b,
        "mod_row": lambda i: jnp.minimum(i * tm // s, b),
        "norm_w": norm_w.reshape(depth * 3, 1, d),
    }

    w13_b, w2_b, w_in_b = ffn_w13.astype(_BF16), ffn_w2.astype(_BF16), w_in.astype(_BF16)
    w_branch_b, w_out_b = w_branch.astype(_BF16), w_out.astype(_BF16)

    cvec = jnp.concatenate([c, c_ctx[None, :], jnp.zeros((8 - b - 1, d), _F32)], axis=0)
    mod = _ada_mod(cvec, w_ada, b_ada).reshape(depth, 8, N_MOD, d)

    cum = jnp.cumsum(jax.nn.softmax(hgrn_lb.astype(_F32), axis=1), axis=1)
    lb = cum - cum[:, :1]
    lb_tab = jnp.stack([jnp.log(lb), jnp.log1p(-lb), 1.0 - lb], axis=2)
    w_tab, lmap = _hgrn_tables()
    cos, sin = _rope_tables(s)

    h = jnp.concatenate([x.reshape(n_lat, d), ctx.reshape(n_ctx, d)], axis=0)
    for layer in range(depth):
        last = layer == depth - 1
        lam_init = 0.8 - 0.6 * math.exp(-0.3 * layer)
        mod_l = mod[layer]
        h = _ffn(h, n_all, mod_l, layer * 3, w13_b, w2_b, layer, 0, 0, dims)
        p = _mixer_in(h, mod_l, layer * 3 + 1, w_in_b, layer, dims)
        qk = _rope(p, cos, sin, b, s, 8 * mix_w)
        o_fb = _hgrn(p, lb_tab[:, layer], w_tab, lmap, b, s, cl, mix_w)
        tq = min(512, s)
        c_att = _attn(qk, 0, 0, tq, s // tq, p, qk, b, s, cl, mix_w, da_lambda, da_norm_w, layer, lam_init, True)
        n_rows = n_lat if last else n_all
        c_ctx_att = None
        if not last:
            c_ctx_att = _attn(p, 8 * mix_w, n_lat, cl, 1, p, None, b, s, cl, mix_w, da_lambda, da_norm_w,
                              layer, lam_init, False)
        a_br, b_br = _branches(o_fb, p, n_rows, hgrn_norm_w, conv_w, layer, b, s, cl, mix_w)
        h = _merge(h, n_rows, mod_l, a_br, b_br, c_att, c_ctx_att, p, w_branch_b, w_out_b, layer, dims, mix_w)
        h = _ffn(h, n_rows, mod_l, layer * 3 + 2, w13_b, w2_b, layer, 1, 6, dims,
                 final_w=final_norm_w if last else None)
    return h.reshape(b, s, d)
```

```python
import functools
import math

import jax
import jax.numpy as jnp
import numpy as np
from jax import lax
from jax.experimental import pallas as pl
from jax.experimental.pallas import tpu as pltpu

_F32 = jnp.float32
_BF16 = jnp.bfloat16

EPS = 1e-6
N_MOD = 9
GRID_W = 64
ROPE_THETA = 10000.0
HEAD_W = 128
DA_HEAD_DIM = 64
LOG2_E = math.log2(math.e)
ATT_Q_SCALE = LOG2_E * DA_HEAD_DIM ** -0.5
ATT_Q_GROUP = 1024
ATT_ONES_ROWS = 16
HG_CHUNK = 64
HG_LEVELS = 6
HG_HEAD_GROUP = 8
FFN_TF = 512
MERGE_TN = 512
ROW_CHUNK = 128
V7X_VMEM_LIMIT = 56 * 1024 * 1024
V7X_VMEM_LIMIT_FFN = 60 * 1024 * 1024


def _dot(a, b):
    return jnp.dot(a, b, preferred_element_type=_F32)


def _dot_nt(a, b):
    return lax.dot_general(a, b, (((1,), (1,)), ((), ())), preferred_element_type=_F32)


def _dot_tn(a, b):
    return lax.dot_general(a, b, (((0,), (0,)), ((), ())), preferred_element_type=_F32)


def _params(semantics, vmem=V7X_VMEM_LIMIT):
    return pltpu.CompilerParams(dimension_semantics=semantics, vmem_limit_bytes=vmem)


def _for_row_chunks(n_rows, body):
    def step(c, carry):
        body(pl.multiple_of(c * ROW_CHUNK, ROW_CHUNK))
        return carry
    lax.fori_loop(0, n_rows // ROW_CHUNK, step, 0)


def _norm_mod(h, nw, shift, scale):
    ms = jnp.mean(h * h, axis=-1, keepdims=True)
    return (h * lax.rsqrt(ms + EPS) * nw) * (1.0 + scale) + shift


def _ada_kernel(c_ref, w_ref, b_ref, o_ref):
    c = c_ref[...]
    s = (c * jax.nn.sigmoid(c)).astype(_BF16)
    o_ref[...] = _dot(s, w_ref[...].astype(_BF16)) + b_ref[...]


def _ada_mod(cvec, w_ada, b_ada, tn=1024):
    depth, d, nd = w_ada.shape
    rows = cvec.shape[0]
    return pl.pallas_call(
        _ada_kernel,
        grid=(depth, nd // tn),
        in_specs=[
            pl.BlockSpec((rows, d), lambda l, j: (0, 0)),
            pl.BlockSpec((None, d, tn), lambda l, j: (l, 0, j)),
            pl.BlockSpec((None, 1, tn), lambda l, j: (l, 0, j)),
        ],
        out_specs=pl.BlockSpec((None, rows, tn), lambda l, j: (l, 0, j)),
        out_shape=jax.ShapeDtypeStruct((depth, rows, nd), _F32),
        compiler_params=_params(("arbitrary", "arbitrary")),
        name="ada_mod",
    )(cvec, w_ada, b_ada.reshape(depth, 1, nd))


def _on_row_tiles(tail_rows, tm, body):
    if tail_rows == 0:
        body(tm)
        return
    is_tail = pl.program_id(0) == pl.num_programs(0) - 1

    @pl.when(jnp.logical_not(is_tail))
    def _():
        body(tm)

    @pl.when(is_tail)
    def _():
        body(tail_rows)


def _ffn_kernel(h_ref, mod_ref, nw_ref, w1_ref, w3_ref, w2_ref, *rest, mod_off, final_norm, tail_rows):
    if final_norm:
        fw_ref, o_ref, xn_ref = rest
    else:
        o_ref, xn_ref = rest
    j = pl.program_id(1)

    def body(n_rows):
        @pl.when(j == 0)
        def _():
            def rows(r0):
                rs = pl.ds(r0, ROW_CHUNK)
                xn = _norm_mod(h_ref[rs, :], nw_ref[...], mod_ref[mod_off:mod_off + 1, :],
                               mod_ref[mod_off + 1:mod_off + 2, :])
                xn_ref[rs, :] = xn.astype(_BF16)
                o_ref[rs, :] = jnp.zeros((ROW_CHUNK, o_ref.shape[1]), o_ref.dtype)
            _for_row_chunks(n_rows, rows)

        xn = xn_ref[0:n_rows, :]
        g = _dot(xn, w1_ref[...])
        u = _dot(xn, w3_ref[...])
        a = (g * jax.nn.sigmoid(g) * u).astype(_BF16)
        o_ref[0:n_rows, :] += _dot(a, w2_ref[...])

        @pl.when(j == pl.num_programs(1) - 1)
        def _():
            gate = mod_ref[mod_off + 2:mod_off + 3, :]

            def rows(r0):
                rs = pl.ds(r0, ROW_CHUNK)
                out = h_ref[rs, :] + 0.5 * gate * o_ref[rs, :]
                if final_norm:
                    ms = jnp.mean(out * out, axis=-1, keepdims=True)
                    out = out * lax.rsqrt(ms + EPS) * fw_ref[...]
                o_ref[rs, :] = out
            _for_row_chunks(n_rows, rows)

    _on_row_tiles(tail_rows, h_ref.shape[0], body)


def _ffn(h, n_rows, mod_l, nw, w13, w2, layer, which, mod_off, dims, final_w=None):
    d = h.shape[1]
    tf = w13.shape[4]
    nfb = w13.shape[2] // 2
    tm = dims["tm_big"]
    mod_row = lambda i: jnp.minimum(i * tm // dims["s"], dims["b"])
    in_specs = [
        pl.BlockSpec((tm, d), lambda i, j: (i, 0)),
        pl.BlockSpec((None, N_MOD, d), lambda i, j: (mod_row(i), 0, 0)),
        pl.BlockSpec((None, 1, d), lambda i, j: (nw, 0, 0)),
        pl.BlockSpec((None, None, None, d, tf), lambda i, j: (layer, which, j, 0, 0)),
        pl.BlockSpec((None, None, None, d, tf), lambda i, j: (layer, which, j + nfb, 0, 0)),
        pl.BlockSpec((None, None, tf, d), lambda i, j: (layer, which, j, 0)),
    ]
    args = [h, mod_l, dims["norm_w"], w13, w13, w2]
    if final_w is not None:
        in_specs.append(pl.BlockSpec((1, d), lambda i, j: (0, 0)))
        args.append(final_w.reshape(1, d))
    return pl.pallas_call(
        functools.partial(_ffn_kernel, mod_off=mod_off, final_norm=final_w is not None, tail_rows=n_rows % tm),
        grid=(pl.cdiv(n_rows, tm), nfb),
        in_specs=in_specs,
        out_specs=pl.BlockSpec((tm, d), lambda i, j: (i, 0)),
        out_shape=jax.ShapeDtypeStruct((n_rows, d), _F32),
        scratch_shapes=[pltpu.VMEM((tm, d), _BF16)],
        compiler_params=_params(("parallel", "arbitrary"), vmem=V7X_VMEM_LIMIT_FFN),
        name="half_ffn",
    )(*args)


def _proj_kernel(h_ref, mod_ref, nw_ref, w_ref, o_ref, xn_ref, *, tail_rows):
    def body(n_rows):
        @pl.when(pl.program_id(1) == 0)
        def _():
            xn = _norm_mod(h_ref[0:n_rows, :], nw_ref[...], mod_ref[3:4, :], mod_ref[4:5, :])
            xn_ref[0:n_rows, :] = xn.astype(_BF16)

        o_ref[0:n_rows, :] = _dot(xn_ref[0:n_rows, :], w_ref[...]).astype(o_ref.dtype)

    _on_row_tiles(tail_rows, h_ref.shape[0], body)


def _mixer_in(h, mod_l, nw, w_in, layer, dims, tn=1024):
    n, d = h.shape
    cols = w_in.shape[2]
    tm = dims["tm_big"]
    mod_row = lambda i: jnp.minimum(i * tm // dims["s"], dims["b"])
    return pl.pallas_call(
        functools.partial(_proj_kernel, tail_rows=n % tm),
        grid=(pl.cdiv(n, tm), cols // tn),
        in_specs=[
            pl.BlockSpec((tm, d), lambda i, j: (i, 0)),
            pl.BlockSpec((None, N_MOD, d), lambda i, j: (mod_row(i), 0, 0)),
            pl.BlockSpec((None, 1, d), lambda i, j: (nw, 0, 0)),
            pl.BlockSpec((None, d, tn), lambda i, j: (layer, 0, j)),
        ],
        out_specs=pl.BlockSpec((tm, tn), lambda i, j: (i, j)),
        out_shape=jax.ShapeDtypeStruct((n, cols), _BF16),
        scratch_shapes=[pltpu.VMEM((tm, d), _BF16)],
        compiler_params=_params(("parallel", "arbitrary")),
        name="mixer_in",
    )(h, mod_l, dims["norm_w"], w_in)


def _rope_kernel(x_ref, cos_ref, sin_ref, o_ref):
    cos = cos_ref[...]
    sin = sin_ref[...]
    lane = lax.broadcasted_iota(jnp.int32, cos.shape, 1)
    first = (lane % 32) < 16
    n_groups = x_ref.shape[1] // HEAD_W
    for g in range(n_groups):
        sl = slice(g * HEAD_W, (g + 1) * HEAD_W)
        x = x_ref[:, sl].astype(_F32)
        rot = jnp.where(first, pltpu.roll(x, HEAD_W - 16, axis=1), pltpu.roll(x, 16, axis=1))
        y = x * cos + rot * sin
        if g < n_groups // 2:
            y = y * ATT_Q_SCALE
        o_ref[:, sl] = y.astype(o_ref.dtype)


def _rope_tables(n_tokens):
    rows = n_tokens // GRID_W
    row = jnp.repeat(jnp.arange(rows, dtype=jnp.int32), GRID_W)
    col = jnp.tile(jnp.arange(GRID_W, dtype=jnp.int32), rows)
    half = DA_HEAD_DIM // 2
    inv_freq = ROPE_THETA ** (-jnp.arange(0, half, 2, dtype=_F32) / half)
    ang_r = row.astype(_F32)[:, None] * inv_freq
    ang_c = col.astype(_F32)[:, None] * inv_freq
    ang = jnp.concatenate([ang_r, ang_r, ang_c, ang_c], axis=-1)
    ang = jnp.concatenate([ang, ang], axis=-1)
    sign = jnp.where((jnp.arange(HEAD_W) % 32) < 16, -1.0, 1.0)
    return jnp.cos(ang), jnp.sin(ang) * sign


def _rope(p, cos, sin, b, s, col0, tr=512):
    width = 2 * dims_mix_w(p)
    tr = min(tr, s)
    return pl.pallas_call(
        _rope_kernel,
        grid=(b * s // tr,),
        in_specs=[
            pl.BlockSpec((tr, width), lambda i: (i, col0 // width)),
            pl.BlockSpec((tr, HEAD_W), lambda i: (i % (s // tr), 0)),
            pl.BlockSpec((tr, HEAD_W), lambda i: (i % (s // tr), 0)),
        ],
        out_specs=pl.BlockSpec((tr, width), lambda i: (i, 0)),
        out_shape=jax.ShapeDtypeStruct((b * s, width), _BF16),
        compiler_params=_params(("parallel",)),
        name="axial_rope",
    )(p, cos, sin)


def dims_mix_w(p):
    return p.shape[1] // 17


def _hgrn_tables():
    L = HG_CHUNK
    w = np.zeros((2, (HG_LEVELS + 2) * L + 8, L), np.float32)
    lmap = np.full((2, L, L), HG_LEVELS + 1, np.int32)
    r = np.arange(L)
    for t in range(L):
        w[0, t, r <= t] = 1
        w[0, L + t, r > t] = 1
        w[1, t, r >= t] = 1
        w[1, L + t, r < t] = 1
        lmap[:, t, t] = 0
        for lvl in range(1, HG_LEVELS + 1):
            bs = (2 * L) >> lvl
            mid = (t // bs) * bs + bs // 2
            base = (lvl + 1) * L + t
            if t >= mid:
                w[0, base, (r >= mid) & (r <= t)] = 1
                w[1, base, (r >= mid) & (r < t)] = 1
            else:
                w[0, base, (r > t) & (r < mid)] = 1
                w[1, base, (r >= t) & (r < mid)] = 1
            for s_ in range(L):
                if s_ // bs == t // bs and t >= mid and s_ < mid:
                    lmap[0, t, s_] = lvl
                    lmap[1, s_, t] = lvl
    w[:, (HG_LEVELS + 2) * L:, :] = 1
    w = np.concatenate([w, w], axis=2)
    return jnp.asarray(w, _BF16), jnp.asarray(lmap)


def _hgrn_kernel(q_ref, z_ref, v_ref, lb_ref, w_ref, lmap_ref, o_ref, st_ref):
    L = HG_CHUNK

    @pl.when(pl.program_id(2) == 0)
    def _():
        st_ref[...] = jnp.zeros_like(st_ref)

    n_heads = q_ref.shape[1] // HEAD_W
    lmap = lmap_ref[...]
    for h0 in range(0, n_heads, HG_HEAD_GROUP):
        gs = slice(h0 * HEAD_W, (h0 + HG_HEAD_GROUP) * HEAD_W)
        heads = [slice(h * HEAD_W, (h + 1) * HEAD_W) for h in range(HG_HEAD_GROUP)]
        z = z_ref[:, gs].astype(_F32)
        ls = jnp.minimum(z, 0.0) - jnp.log(1.0 + jnp.exp(-jnp.abs(z)))
        bb = lb_ref[1:2, gs] + ls
        log_lb = lb_ref[0:1, gs]
        lf = jnp.maximum(log_lb, bb) + jnp.log(1.0 + jnp.exp(-jnp.abs(log_lb - bb)))
        k = lb_ref[2:3, gs] * jnp.exp(ls - z)
        lf2 = lf * LOG2_E
        lf_hi = lf2.astype(_BF16)
        lf_lo = (lf2 - lf_hi.astype(_F32)).astype(_BF16)
        e = jnp.exp2(_dot(w_ref[...], jnp.concatenate([lf_hi, lf_lo], axis=0)))
        q = q_ref[:, gs].astype(_F32)
        v = v_ref[:, gs]
        qk = q * k
        a = [jnp.where(lmap == 0, jnp.sum(qk[:, sl], axis=-1, keepdims=True) * jnp.ones((1, L), _F32), 0.0)
             for sl in heads]
        for lvl in range(1, HG_LEVELS + 1):
            el = e[(lvl + 1) * L:(lvl + 2) * L]
            ql = (q * el).astype(_BF16)
            kl = (k * el).astype(_BF16)
            a = [jnp.where(lmap == lvl, _dot_nt(ql[:, sl], kl[:, sl]), a[h]) for h, sl in enumerate(heads)]
        q_in = (q * e[0:L]).astype(_BF16)
        k_out = (k * e[L:2 * L]).astype(_BF16)
        tot = e[(HG_LEVELS + 2) * L:(HG_LEVELS + 2) * L + 1]
        for h, sl in enumerate(heads):
            st = st_ref[h0 + h]
            osl = slice((h0 + h) * HEAD_W, (h0 + h + 1) * HEAD_W)
            o_ref[:, osl] = _dot(a[h].astype(_BF16), v[:, sl]) + _dot_nt(q_in[:, sl], st.astype(_BF16))
            st_ref[h0 + h] = tot[:, sl] * st + _dot_tn(v[:, sl], k_out[:, sl])


def _hgrn(p, lb_tab, w_tab, lmap, b, s, c, mix_w):
    n = p.shape[0]
    L = HG_CHUNK
    ncc, ncl = c // L, s // L
    nheads = mix_w // HEAD_W

    def row_block(bi, di, j):
        is_ctx = j < ncc
        cj = jnp.where(di == 0, j, ncc - 1 - j)
        lj = jnp.where(di == 0, j - ncc, ncl - 1 - (j - ncc))
        return jnp.where(is_ctx, (b * s + bi * c) // L + cj, bi * ncl + lj)

    def spec(col):
        return pl.BlockSpec((L, mix_w), lambda bi, di, j: (row_block(bi, di, j), col))

    return pl.pallas_call(
        _hgrn_kernel,
        grid=(b, 2, ncc + ncl),
        in_specs=[
            spec(0),
            pl.BlockSpec((L, mix_w), lambda bi, di, j: (row_block(bi, di, j), 1 + di)),
            spec(3),
            pl.BlockSpec((None, 3, mix_w), lambda bi, di, j: (di, 0, 0)),
            pl.BlockSpec((None,) + w_tab.shape[1:], lambda bi, di, j: (di, 0, 0)),
            pl.BlockSpec((None, L, L), lambda bi, di, j: (di, 0, 0)),
        ],
        out_specs=pl.BlockSpec((None, L, mix_w), lambda bi, di, j: (di, row_block(bi, di, j), 0)),
        out_shape=jax.ShapeDtypeStruct((2, n, mix_w), _F32),
        scratch_shapes=[pltpu.VMEM((nheads, HEAD_W, HEAD_W), _F32)],
        compiler_params=_params(("arbitrary", "arbitrary", "arbitrary")),
        name="hgrn_scan",
    )(p, p, p, lb_tab, w_tab, lmap)


def _attn_kernel(*refs, has_lat, tk, lam_init, q_scale):
    if has_lat:
        (q_ref, kc_ref, vc_ref, kl_ref, vl_ref, lamp_ref, nw_ref, o_ref, m_ref, acc_ref, vtc_ref, vtl_ref,
         st_ref, mx_ref) = refs
    else:
        q_ref, kc_ref, vc_ref, lamp_ref, nw_ref, o_ref, m_ref, acc_ref, vtc_ref = refs
    tq = q_ref.shape[0]
    eye = (lax.broadcasted_iota(jnp.int32, (HEAD_W, HEAD_W), 0)
           == lax.broadcasted_iota(jnp.int32, (HEAD_W, HEAD_W), 1)).astype(_F32).astype(_BF16)

    @pl.when(pl.program_id(2) == 0)
    def _():
        ones = jnp.ones((ATT_ONES_ROWS, vtc_ref.shape[1]), _BF16)
        vtc_ref[0:HEAD_W, :] = _dot_nt(eye, vc_ref[...]).astype(_BF16)
        vtc_ref[HEAD_W:, :] = ones
        if has_lat:
            def tbody(i, carry):
                off = pl.multiple_of(i * tk, tk)
                vtl_ref[0:HEAD_W, pl.ds(off, tk)] = _dot_nt(eye, vl_ref[pl.ds(off, tk), :]).astype(_BF16)
                vtl_ref[HEAD_W:, pl.ds(off, tk)] = jnp.ones((ATT_ONES_ROWS, tk), _BF16)
                return carry
            lax.fori_loop(0, vl_ref.shape[0] // tk, tbody, 0)

    q = q_ref[...].astype(_F32)
    if q_scale is not None:
        q = q * q_scale
    lane = lax.broadcasted_iota(jnp.int32, q.shape, 1)
    qq = jnp.concatenate([jnp.where(lane < DA_HEAD_DIM, q, 0.0), jnp.where(lane >= DA_HEAD_DIM, q, 0.0)],
                         axis=0).astype(_BF16)

    m_ref[...] = jnp.full_like(m_ref, -jnp.inf)
    acc_ref[...] = jnp.zeros_like(acc_ref)

    def softmax_pv(st, st_max, vt):
        group = min(ATT_Q_GROUP, 2 * tq)
        for c0 in range(0, 2 * tq, group):
            cs = slice(c0, c0 + group)
            m_prev = m_ref[:, cs]
            m_new = jnp.maximum(m_prev, st_max[:, cs])
            alpha = jnp.exp2(m_prev - m_new)
            pt = jnp.exp2(st[:, cs] - m_new).astype(_BF16)
            acc_ref[:, cs] = alpha * acc_ref[:, cs] + _dot(vt, pt)
            m_ref[:, cs] = m_new

    def scores(t, buf):
        off = pl.multiple_of((t - 1) * tk, tk)
        st = _dot_nt(kl_ref[pl.ds(off, tk), :], qq)
        st_ref[buf] = st
        mx_ref[buf] = jnp.max(st, axis=0, keepdims=True)

    st_c = _dot_nt(kc_ref[...], qq)
    mx_c = jnp.max(st_c, axis=0, keepdims=True)
    if not has_lat:
        softmax_pv(st_c, mx_c, vtc_ref[...])
    else:
        n = kl_ref.shape[0] // tk

        def consume(t, buf):
            off = pl.multiple_of((t - 1) * tk, tk)
            softmax_pv(st_ref[buf], mx_ref[buf], vtl_ref[:, pl.ds(off, tk)])

        scores(1, 1)
        softmax_pv(st_c, mx_c, vtc_ref[...])
        n_pairs = (n - 1) // 2

        def body(j, carry):
            t = 2 * j + 1
            scores(t + 1, 0)
            consume(t, 1)
            scores(t + 2, 1)
            consume(t + 1, 0)
            return carry
        lax.fori_loop(0, n_pairs, body, 0)
        for t in range(2 * n_pairs + 1, n + 1):
            if t < n:
                scores(t + 1, (t + 1) % 2)
            consume(t, t % 2)

    lp = lamp_ref[...]
    lam = (jnp.exp(jnp.sum(lp[0:1] * lp[1:2], axis=-1, keepdims=True))
           - jnp.exp(jnp.sum(lp[2:3] * lp[3:4], axis=-1, keepdims=True)) + lam_init)
    on = acc_ref[0:HEAD_W, :] / acc_ref[HEAD_W:HEAD_W + 1, :]
    ot = on[:, 0:tq] - lam * on[:, tq:2 * tq]
    ms = jnp.mean(ot * ot, axis=0, keepdims=True)
    o = (ot * lax.rsqrt(ms + EPS)).T
    o_ref[...] = (o * nw_ref[...] * (1.0 - lam_init)).astype(o_ref.dtype)


def _attn(q_arr, q_col0, q_row0, tq, nq, p, kl_arr, b, s, c, mix_w, lam_p, nw, layer, lam_init, has_lat, tk=1024):
    nheads = mix_w // HEAD_W
    ak0, av0 = 9 * mix_w // HEAD_W, 10 * mix_w // HEAD_W
    ctx_blk0 = b * s // c
    in_specs = [
        pl.BlockSpec((tq, HEAD_W), lambda bi, h, qi: (q_row0 // tq + bi * nq + qi, q_col0 // HEAD_W + h)),
        pl.BlockSpec((c, HEAD_W), lambda bi, h, qi: (ctx_blk0 + bi, ak0 + h)),
        pl.BlockSpec((c, HEAD_W), lambda bi, h, qi: (ctx_blk0 + bi, av0 + h)),
    ]
    args = [q_arr, p, p]
    if has_lat:
        in_specs += [
            pl.BlockSpec((s, HEAD_W), lambda bi, h, qi: (bi, nheads + h)),
            pl.BlockSpec((s, HEAD_W), lambda bi, h, qi: (bi, av0 + h)),
        ]
        args += [kl_arr, p]
    in_specs += [
        pl.BlockSpec((None, 4, DA_HEAD_DIM), lambda bi, h, qi: (layer, 0, 0)),
        pl.BlockSpec((None, 1, HEAD_W), lambda bi, h, qi: (layer, 0, 0)),
    ]
    args += [lam_p, nw.reshape(-1, 1, HEAD_W)]
    vt_rows = HEAD_W + ATT_ONES_ROWS
    scratch = [pltpu.VMEM((1, 2 * tq), _F32), pltpu.VMEM((vt_rows, 2 * tq), _F32), pltpu.VMEM((vt_rows, c), _BF16)]
    tk = min(tk, s)
    if has_lat:
        scratch += [pltpu.VMEM((vt_rows, s), _BF16), pltpu.VMEM((2, tk, 2 * tq), _F32),
                    pltpu.VMEM((2, 1, 2 * tq), _F32)]
    return pl.pallas_call(
        functools.partial(_attn_kernel, has_lat=has_lat, tk=tk, lam_init=lam_init,
                          q_scale=None if has_lat else ATT_Q_SCALE),
        grid=(b, nheads, nq),
        in_specs=in_specs,
        out_specs=pl.BlockSpec((tq, HEAD_W), lambda bi, h, qi: (bi * nq + qi, h)),
        out_shape=jax.ShapeDtypeStruct((b * nq * tq, mix_w), _BF16),
        scratch_shapes=scratch,
        compiler_params=_params(("arbitrary", "arbitrary", "arbitrary")),
        name="diff_attn_lat" if has_lat else "diff_attn_ctx",
    )(*args)


def _branch_kernel(of_ref, ob_ref, hg_ref, cb_ref, cc_ref, cu_ref, ccp_ref, cup_ref, ccn_ref, cun_ref,
                   hnw_ref, cw_ref, a_ref, b_ref, *, n_lat, s, c):
    tm = of_ref.shape[0]
    nw = hnw_ref[...]
    for h in range(of_ref.shape[1] // HEAD_W):
        sl = slice(h * HEAD_W, (h + 1) * HEAD_W)
        o = of_ref[:, sl] + ob_ref[:, sl]
        ms = jnp.mean(o * o, axis=-1, keepdims=True)
        g = hg_ref[:, sl].astype(_F32)
        a_ref[:, sl] = ((o * lax.rsqrt(ms + EPS) * nw) * (g * jax.nn.sigmoid(g))).astype(a_ref.dtype)
    row = pl.program_id(0) * tm + lax.broadcasted_iota(jnp.int32, (tm, 1), 0)
    pos = jnp.where(row < n_lat, row % s, (row - n_lat) % c)
    seq_len = jnp.where(row < n_lat, s, c)
    local = lax.broadcasted_iota(jnp.int32, (tm, 1), 0)
    x = cc_ref[...].astype(_F32) * cu_ref[...].astype(_F32)
    x_halo_p = ccp_ref[7:8, :].astype(_F32) * cup_ref[7:8, :].astype(_F32)
    x_halo_n = ccn_ref[0:1, :].astype(_F32) * cun_ref[0:1, :].astype(_F32)
    x_prev = jnp.where(local == 0, x_halo_p, pltpu.roll(x, 1, axis=0))
    x_prev = jnp.where(pos == 0, 0.0, x_prev)
    x_next = jnp.where(local == tm - 1, x_halo_n, pltpu.roll(x, tm - 1, axis=0))
    x_next = jnp.where(pos == seq_len - 1, 0.0, x_next)
    y = cw_ref[0:1, :] * x_prev + cw_ref[1:2, :] * x + cw_ref[2:3, :] * x_next
    b_ref[...] = (cb_ref[...].astype(_F32) * y).astype(b_ref.dtype)


def _branches(o_fb, p, n_rows, hg_nw, conv_w, layer, b, s, c, mix_w, tm=256):
    tm = math.gcd(math.gcd(tm, s), b * c)
    last8 = p.shape[0] // 8 - 1
    main = lambda col: pl.BlockSpec((tm, mix_w), lambda i: (i, col))
    prev = lambda col: pl.BlockSpec((8, mix_w), lambda i: (jnp.maximum(i * (tm // 8) - 1, 0), col))
    nxt = lambda col: pl.BlockSpec((8, mix_w), lambda i: (jnp.minimum((i + 1) * (tm // 8), last8), col))
    out_sds = jax.ShapeDtypeStruct((n_rows, mix_w), _BF16)
    return pl.pallas_call(
        functools.partial(_branch_kernel, n_lat=b * s, s=s, c=c),
        grid=(n_rows // tm,),
        in_specs=[
            pl.BlockSpec((None, tm, mix_w), lambda i: (0, i, 0)),
            pl.BlockSpec((None, tm, mix_w), lambda i: (1, i, 0)),
            main(4), main(5), main(6), main(7), prev(6), prev(7), nxt(6), nxt(7),
            pl.BlockSpec((None, 1, HEAD_W), lambda i: (layer, 0, 0)),
            pl.BlockSpec((None, 3, mix_w), lambda i: (layer, 0, 0)),
        ],
        out_specs=[pl.BlockSpec((tm, mix_w), lambda i: (i, 0)), pl.BlockSpec((tm, mix_w), lambda i: (i, 0))],
        out_shape=[out_sds, out_sds],
        compiler_params=_params(("parallel",)),
        name="branch_prep",
    )(o_fb, o_fb, p, p, p, p, p, p, p, p, hg_nw.reshape(-1, 1, HEAD_W), conv_w)


def _merge_kernel(h_ref, mod_ref, a_ref, b_ref, c_ref, cx_ref, ga_ref, gb_ref, gc_ref, wbr_ref, wo_ref, o_ref, *,
                  n_lat_tiles):
    j = pl.program_id(1)
    tn = ga_ref.shape[1]

    @pl.when(j == 0)
    def _():
        o_ref[...] = jnp.zeros_like(o_ref)

    def gated(g_ref, x, k):
        return jax.nn.sigmoid(g_ref[...].astype(_F32)) * _dot(x, wbr_ref[k, j])

    c = jnp.where(pl.program_id(0) >= n_lat_tiles, cx_ref[...], c_ref[...])
    y = gated(ga_ref, a_ref[...], 0) + gated(gb_ref, b_ref[...], 1) + gated(gc_ref, c, 2)
    o_ref[...] += _dot(y.astype(_BF16), wo_ref[pl.ds(pl.multiple_of(j * tn, tn), tn), :])

    @pl.when(j == pl.num_programs(1) - 1)
    def _():
        o_ref[...] = h_ref[...] + mod_ref[5:6, :] * o_ref[...]


def _merge(h, n_rows, mod_l, a, bb, c_lat, c_ctx, p, w_branch, w_out, layer, dims, mix_w):
    d = h.shape[1]
    tn = w_branch.shape[4]
    tm = dims["tm"]
    mod_row = dims["mod_row"]
    g0 = 11 * mix_w // tn
    gstep = d // tn
    n_lat_tiles = c_lat.shape[0] // tm
    if c_ctx is None:
        c_ctx = c_lat
    br = lambda: pl.BlockSpec((tm, mix_w), lambda i, j: (i, 0))
    gate = lambda k: pl.BlockSpec((tm, tn), lambda i, j: (i, g0 + k * gstep + j))
    return pl.pallas_call(
        functools.partial(_merge_kernel, n_lat_tiles=n_lat_tiles),
        grid=(n_rows // tm, d // tn),
        in_specs=[
            pl.BlockSpec((tm, d), lambda i, j: (i, 0)),
            pl.BlockSpec((None, N_MOD, d), lambda i, j: (mod_row(i), 0, 0)),
            br(), br(),
            pl.BlockSpec((tm, mix_w), lambda i, j: (jnp.minimum(i, n_lat_tiles - 1), 0)),
            pl.BlockSpec((tm, mix_w), lambda i, j: (jnp.maximum(i - n_lat_tiles, 0), 0)),
            gate(0), gate(1), gate(2),
            pl.BlockSpec((None,) + w_branch.shape[1:], lambda i, j: (layer, 0, 0, 0, 0),
                         pipeline_mode=pl.Buffered(1)),
            pl.BlockSpec((None, d, d), lambda i, j: (layer, 0, 0), pipeline_mode=pl.Buffered(1)),
        ],
        out_specs=pl.BlockSpec((tm, d), lambda i, j: (i, 0)),
        out_shape=jax.ShapeDtypeStruct((n_rows, d), _F32),
        compiler_params=_params(("parallel", "arbitrary")),
        name="merge_out",
    )(h, mod_l, a, bb, c_lat, c_ctx, p, p, p, w_branch, w_out)


def kernel(x, c, ctx, c_ctx, w_ada, b_ada, norm_w, ffn_w13, ffn_w2, w_in, hgrn_lb, hgrn_norm_w, conv_w,
           da_lambda, da_norm_w, w_branch, w_out, final_norm_w):
    b, s, d = x.shape
    cl = ctx.shape[1]
    depth = w_ada.shape[0]
    mix_w = d // 2
    n_lat, n_ctx = b * s, b * cl
    n_all = n_lat + n_ctx
    tm = math.gcd(math.gcd(512, s), n_ctx)
    dims = {
        "tm": tm,
        "tm_big": math.gcd(1024, s),
        "s": s,
        "b": b,
        "mod_row": lambda i: jnp.minimum(i * tm // s, b),
        "norm_w": norm_w.reshape(depth * 3, 1, d),
    }

    w2_b, w_in_b, w_out_b = ffn_w2.astype(_BF16), w_in.astype(_BF16), w_out.astype(_BF16)
    w13_b = (ffn_w13.astype(_BF16).reshape(depth, 2, d, ffn_w13.shape[3] // FFN_TF, FFN_TF)
             .transpose(0, 1, 3, 2, 4))
    w_branch_b = (w_branch.astype(_BF16).reshape(depth, w_branch.shape[1], mix_w, d // MERGE_TN, MERGE_TN)
                  .transpose(0, 1, 3, 2, 4))

    cvec = jnp.concatenate([c, c_ctx[None, :], jnp.zeros((8 - b - 1, d), _F32)], axis=0)
    mod = _ada_mod(cvec, w_ada, b_ada).reshape(depth, 8, N_MOD, d)

    cum = jnp.cumsum(jax.nn.softmax(hgrn_lb.astype(_F32), axis=1), axis=1)
    lb = cum - cum[:, :1]
    lb_tab = jnp.stack([jnp.log(lb), jnp.log1p(-lb), 1.0 - lb], axis=2)
    w_tab, lmap = _hgrn_tables()
    cos, sin = _rope_tables(s)

    h = jnp.concatenate([x.reshape(n_lat, d), ctx.reshape(n_ctx, d)], axis=0)
    for layer in range(depth):
        last = layer == depth - 1
        lam_init = 0.8 - 0.6 * math.exp(-0.3 * layer)
        mod_l = mod[layer]
        h = _ffn(h, n_all, mod_l, layer * 3, w13_b, w2_b, layer, 0, 0, dims)
        p = _mixer_in(h, mod_l, layer * 3 + 1, w_in_b, layer, dims)
        qk = _rope(p, cos, sin, b, s, 8 * mix_w)
        o_fb = _hgrn(p, lb_tab[:, layer], w_tab, lmap, b, s, cl, mix_w)
        tq = min(512, s)
        c_att = _attn(qk, 0, 0, tq, s // tq, p, qk, b, s, cl, mix_w, da_lambda, da_norm_w, layer, lam_init, True)
        n_rows = n_lat if last else n_all
        c_ctx_att = None
        if not last:
            c_ctx_att = _attn(p, 8 * mix_w, n_lat, cl, 1, p, None, b, s, cl, mix_w, da_lambda, da_norm_w,
                              layer, lam_init, False)
        a_br, b_br = _branches(o_fb, p, n_rows, hgrn_norm_w, conv_w, layer, b, s, cl, mix_w)
        h = _merge(h, n_rows, mod_l, a_br, b_br, c_att, c_ctx_att, p, w_branch_b, w_out_b, layer, dims, mix_w)
        h = _ffn(h, n_rows, mod_l, layer * 3 + 2, w13_b, w2_b, layer, 1, 6, dims,
                 final_w=final_norm_w if last else None)
    return h.reshape(b, s, d)
```

```python
import functools
import math

import jax
import jax.numpy as jnp
import numpy as np
from jax import lax
from jax.experimental import pallas as pl
from jax.experimental.pallas import tpu as pltpu

_F32 = jnp.float32
_BF16 = jnp.bfloat16

EPS = 1e-6
N_MOD = 9
GRID_W = 64
ROPE_THETA = 10000.0
HEAD_W = 128
DA_HEAD_DIM = 64
LOG2_E = math.log2(math.e)
ATT_Q_SCALE = LOG2_E * DA_HEAD_DIM ** -0.5
ATT_ONES_ROWS = 16
HG_CHUNK = 64
HG_LEVELS = 6
FFN_TF = 512
MERGE_TN = 512
ROW_CHUNK = 128
V7X_VMEM_LIMIT = 56 * 1024 * 1024
V7X_VMEM_LIMIT_FFN = 60 * 1024 * 1024


def _dot(a, b):
    return jnp.dot(a, b, preferred_element_type=_F32)


def _dot_nt(a, b):
    return lax.dot_general(a, b, (((1,), (1,)), ((), ())), preferred_element_type=_F32)


def _dot_tn(a, b):
    return lax.dot_general(a, b, (((0,), (0,)), ((), ())), preferred_element_type=_F32)


def _params(semantics, vmem=V7X_VMEM_LIMIT):
    return pltpu.CompilerParams(dimension_semantics=semantics, vmem_limit_bytes=vmem)


def _for_row_chunks(n_rows, body):
    def step(c, carry):
        body(pl.multiple_of(c * ROW_CHUNK, ROW_CHUNK))
        return carry
    lax.fori_loop(0, n_rows // ROW_CHUNK, step, 0)


def _norm_mod(h, nw, shift, scale):
    ms = jnp.mean(h * h, axis=-1, keepdims=True)
    return (h * lax.rsqrt(ms + EPS) * nw) * (1.0 + scale) + shift


def _ada_kernel(c_ref, w_ref, b_ref, o_ref):
    c = c_ref[...]
    s = (c * jax.nn.sigmoid(c)).astype(_BF16)
    o_ref[...] = _dot(s, w_ref[...].astype(_BF16)) + b_ref[...]


def _ada_mod(cvec, w_ada, b_ada, tn=1024):
    depth, d, nd = w_ada.shape
    rows = cvec.shape[0]
    return pl.pallas_call(
        _ada_kernel,
        grid=(depth, nd // tn),
        in_specs=[
            pl.BlockSpec((rows, d), lambda l, j: (0, 0)),
            pl.BlockSpec((None, d, tn), lambda l, j: (l, 0, j)),
            pl.BlockSpec((None, 1, tn), lambda l, j: (l, 0, j)),
        ],
        out_specs=pl.BlockSpec((None, rows, tn), lambda l, j: (l, 0, j)),
        out_shape=jax.ShapeDtypeStruct((depth, rows, nd), _F32),
        compiler_params=_params(("arbitrary", "arbitrary")),
        name="ada_mod",
    )(cvec, w_ada, b_ada.reshape(depth, 1, nd))


def _on_row_tiles(tail_rows, tm, body):
    if tail_rows == 0:
        body(tm)
        return
    is_tail = pl.program_id(0) == pl.num_programs(0) - 1

    @pl.when(jnp.logical_not(is_tail))
    def _():
        body(tm)

    @pl.when(is_tail)
    def _():
        body(tail_rows)


def _ffn_kernel(h_ref, mod_ref, nw_ref, w1_ref, w3_ref, w2_ref, *rest, mod_off, final_norm, tail_rows):
    if final_norm:
        fw_ref, o_ref, xn_ref = rest
    else:
        o_ref, xn_ref = rest
    j = pl.program_id(1)

    def body(n_rows):
        @pl.when(j == 0)
        def _():
            def rows(r0):
                rs = pl.ds(r0, ROW_CHUNK)
                xn = _norm_mod(h_ref[rs, :], nw_ref[...], mod_ref[mod_off:mod_off + 1, :],
                               mod_ref[mod_off + 1:mod_off + 2, :])
                xn_ref[rs, :] = xn.astype(_BF16)
                o_ref[rs, :] = jnp.zeros((ROW_CHUNK, o_ref.shape[1]), o_ref.dtype)
            _for_row_chunks(n_rows, rows)

        xn = xn_ref[0:n_rows, :]
        g = _dot(xn, w1_ref[...])
        u = _dot(xn, w3_ref[...])
        a = (g * jax.nn.sigmoid(g) * u).astype(_BF16)
        o_ref[0:n_rows, :] += _dot(a, w2_ref[...])

        @pl.when(j == pl.num_programs(1) - 1)
        def _():
            gate = mod_ref[mod_off + 2:mod_off + 3, :]

            def rows(r0):
                rs = pl.ds(r0, ROW_CHUNK)
                out = h_ref[rs, :] + 0.5 * gate * o_ref[rs, :]
                if final_norm:
                    ms = jnp.mean(out * out, axis=-1, keepdims=True)
                    out = out * lax.rsqrt(ms + EPS) * fw_ref[...]
                o_ref[rs, :] = out
            _for_row_chunks(n_rows, rows)

    _on_row_tiles(tail_rows, h_ref.shape[0], body)


def _ffn(h, n_rows, mod_l, nw, w13, w2, layer, which, mod_off, dims, final_w=None, tf=FFN_TF):
    d = h.shape[1]
    nfb = w2.shape[2] // tf
    tm = dims["tm_big"]
    mod_row = lambda i: jnp.minimum(i * tm // dims["s"], dims["b"])
    in_specs = [
        pl.BlockSpec((tm, d), lambda i, j: (i, 0)),
        pl.BlockSpec((None, N_MOD, d), lambda i, j: (mod_row(i), 0, 0)),
        pl.BlockSpec((None, 1, d), lambda i, j: (nw, 0, 0)),
        pl.BlockSpec((None, None, d, tf), lambda i, j: (layer, which, 0, j)),
        pl.BlockSpec((None, None, d, tf), lambda i, j: (layer, which, 0, j + nfb)),
        pl.BlockSpec((None, None, tf, d), lambda i, j: (layer, which, j, 0)),
    ]
    args = [h, mod_l, dims["norm_w"], w13, w13, w2]
    if final_w is not None:
        in_specs.append(pl.BlockSpec((1, d), lambda i, j: (0, 0)))
        args.append(final_w.reshape(1, d))
    return pl.pallas_call(
        functools.partial(_ffn_kernel, mod_off=mod_off, final_norm=final_w is not None, tail_rows=n_rows % tm),
        grid=(pl.cdiv(n_rows, tm), nfb),
        in_specs=in_specs,
        out_specs=pl.BlockSpec((tm, d), lambda i, j: (i, 0)),
        out_shape=jax.ShapeDtypeStruct((n_rows, d), _F32),
        scratch_shapes=[pltpu.VMEM((tm, d), _BF16)],
        compiler_params=_params(("parallel", "arbitrary"), vmem=V7X_VMEM_LIMIT_FFN),
        name="half_ffn",
    )(*args)


def _proj_kernel(h_ref, mod_ref, nw_ref, w_ref, o_ref, xn_ref, *, tail_rows):
    def body(n_rows):
        @pl.when(pl.program_id(1) == 0)
        def _():
            xn = _norm_mod(h_ref[0:n_rows, :], nw_ref[...], mod_ref[3:4, :], mod_ref[4:5, :])
            xn_ref[0:n_rows, :] = xn.astype(_BF16)

        o_ref[0:n_rows, :] = _dot(xn_ref[0:n_rows, :], w_ref[...]).astype(o_ref.dtype)

    _on_row_tiles(tail_rows, h_ref.shape[0], body)


def _mixer_in(h, mod_l, nw, w_in, layer, dims, tn=1024):
    n, d = h.shape
    cols = w_in.shape[2]
    tm = dims["tm_big"]
    mod_row = lambda i: jnp.minimum(i * tm // dims["s"], dims["b"])
    return pl.pallas_call(
        functools.partial(_proj_kernel, tail_rows=n % tm),
        grid=(pl.cdiv(n, tm), cols // tn),
        in_specs=[
            pl.BlockSpec((tm, d), lambda i, j: (i, 0)),
            pl.BlockSpec((None, N_MOD, d), lambda i, j: (mod_row(i), 0, 0)),
            pl.BlockSpec((None, 1, d), lambda i, j: (nw, 0, 0)),
            pl.BlockSpec((None, d, tn), lambda i, j: (layer, 0, j)),
        ],
        out_specs=pl.BlockSpec((tm, tn), lambda i, j: (i, j)),
        out_shape=jax.ShapeDtypeStruct((n, cols), _BF16),
        scratch_shapes=[pltpu.VMEM((tm, d), _BF16)],
        compiler_params=_params(("parallel", "arbitrary")),
        name="mixer_in",
    )(h, mod_l, dims["norm_w"], w_in)


def _rope_kernel(x_ref, cos_ref, sin_ref, o_ref):
    cos = cos_ref[...]
    sin = sin_ref[...]
    lane = lax.broadcasted_iota(jnp.int32, cos.shape, 1)
    first = (lane % 32) < 16
    n_groups = x_ref.shape[1] // HEAD_W
    for g in range(n_groups):
        sl = slice(g * HEAD_W, (g + 1) * HEAD_W)
        x = x_ref[:, sl].astype(_F32)
        rot = jnp.where(first, pltpu.roll(x, HEAD_W - 16, axis=1), pltpu.roll(x, 16, axis=1))
        y = x * cos + rot * sin
        if g < n_groups // 2:
            y = y * ATT_Q_SCALE
        o_ref[:, sl] = y.astype(o_ref.dtype)


def _rope_tables(n_tokens):
    rows = n_tokens // GRID_W
    row = jnp.repeat(jnp.arange(rows, dtype=jnp.int32), GRID_W)
    col = jnp.tile(jnp.arange(GRID_W, dtype=jnp.int32), rows)
    half = DA_HEAD_DIM // 2
    inv_freq = ROPE_THETA ** (-jnp.arange(0, half, 2, dtype=_F32) / half)
    ang_r = row.astype(_F32)[:, None] * inv_freq
    ang_c = col.astype(_F32)[:, None] * inv_freq
    ang = jnp.concatenate([ang_r, ang_r, ang_c, ang_c], axis=-1)
    ang = jnp.concatenate([ang, ang], axis=-1)
    sign = jnp.where((jnp.arange(HEAD_W) % 32) < 16, -1.0, 1.0)
    return jnp.cos(ang), jnp.sin(ang) * sign


def _rope(p, cos, sin, b, s, col0, tr=512):
    width = 2 * dims_mix_w(p)
    tr = min(tr, s)
    return pl.pallas_call(
        _rope_kernel,
        grid=(b * s // tr,),
        in_specs=[
            pl.BlockSpec((tr, width), lambda i: (i, col0 // width)),
            pl.BlockSpec((tr, HEAD_W), lambda i: (i % (s // tr), 0)),
            pl.BlockSpec((tr, HEAD_W), lambda i: (i % (s // tr), 0)),
        ],
        out_specs=pl.BlockSpec((tr, width), lambda i: (i, 0)),
        out_shape=jax.ShapeDtypeStruct((b * s, width), _BF16),
        compiler_params=_params(("parallel",)),
        name="axial_rope",
    )(p, cos, sin)


def dims_mix_w(p):
    return p.shape[1] // 17


def _hgrn_tables():
    L = HG_CHUNK
    w = np.zeros((2, (HG_LEVELS + 2) * L + 8, L), np.float32)
    lmap = np.full((2, L, L), HG_LEVELS + 1, np.int32)
    r = np.arange(L)
    for t in range(L):
        w[0, t, r <= t] = 1
        w[0, L + t, r > t] = 1
        w[1, t, r >= t] = 1
        w[1, L + t, r < t] = 1
        lmap[:, t, t] = 0
        for lvl in range(1, HG_LEVELS + 1):
            bs = (2 * L) >> lvl
            mid = (t // bs) * bs + bs // 2
            base = (lvl + 1) * L + t
            if t >= mid:
                w[0, base, (r >= mid) & (r <= t)] = 1
                w[1, base, (r >= mid) & (r < t)] = 1
            else:
                w[0, base, (r > t) & (r < mid)] = 1
                w[1, base, (r >= t) & (r < mid)] = 1
            for s_ in range(L):
                if s_ // bs == t // bs and t >= mid and s_ < mid:
                    lmap[0, t, s_] = lvl
                    lmap[1, s_, t] = lvl
    w[:, (HG_LEVELS + 2) * L:, :] = 1
    w = np.concatenate([w, w], axis=2)
    return jnp.asarray(w, _BF16), jnp.asarray(lmap)


def _hgrn_kernel(q_ref, z_ref, v_ref, lb_ref, w_ref, lmap_ref, o_ref, st_ref):
    L = HG_CHUNK

    @pl.when(pl.program_id(2) == 0)
    def _():
        st_ref[...] = jnp.zeros_like(st_ref)

    heads = [slice(h * HEAD_W, (h + 1) * HEAD_W) for h in range(q_ref.shape[1] // HEAD_W)]
    lmap = lmap_ref[...]
    z = z_ref[...].astype(_F32)
    ls = jnp.minimum(z, 0.0) - jnp.log(1.0 + jnp.exp(-jnp.abs(z)))
    bb = lb_ref[1:2, :] + ls
    log_lb = lb_ref[0:1, :]
    lf = jnp.maximum(log_lb, bb) + jnp.log(1.0 + jnp.exp(-jnp.abs(log_lb - bb)))
    k = lb_ref[2:3, :] * jnp.exp(ls - z)
    lf2 = lf * LOG2_E
    lf_hi = lf2.astype(_BF16)
    lf_lo = (lf2 - lf_hi.astype(_F32)).astype(_BF16)
    e = jnp.exp2(_dot(w_ref[...], jnp.concatenate([lf_hi, lf_lo], axis=0)))
    q = q_ref[...].astype(_F32)
    v = v_ref[...]
    qk = q * k
    a = [jnp.where(lmap == 0, jnp.sum(qk[:, sl], axis=-1, keepdims=True) * jnp.ones((1, L), _F32), 0.0)
         for sl in heads]
    for lvl in range(1, HG_LEVELS + 1):
        el = e[(lvl + 1) * L:(lvl + 2) * L]
        ql = (q * el).astype(_BF16)
        kl = (k * el).astype(_BF16)
        a = [jnp.where(lmap == lvl, _dot_nt(ql[:, sl], kl[:, sl]), a[h]) for h, sl in enumerate(heads)]
    q_in = (q * e[0:L]).astype(_BF16)
    k_out = (k * e[L:2 * L]).astype(_BF16)
    tot = e[(HG_LEVELS + 2) * L:(HG_LEVELS + 2) * L + 1]
    for h, sl in enumerate(heads):
        st = st_ref[h]
        o_ref[:, sl] = _dot(a[h].astype(_BF16), v[:, sl]) + _dot_nt(q_in[:, sl], st.astype(_BF16))
        st_ref[h] = tot[:, sl] * st + _dot_tn(v[:, sl], k_out[:, sl])


def _hgrn(p, lb_tab, w_tab, lmap, b, s, c, mix_w):
    n = p.shape[0]
    L = HG_CHUNK
    ncc, ncl = c // L, s // L
    nheads = mix_w // HEAD_W

    def row_block(bi, di, j):
        is_ctx = j < ncc
        cj = jnp.where(di == 0, j, ncc - 1 - j)
        lj = jnp.where(di == 0, j - ncc, ncl - 1 - (j - ncc))
        return jnp.where(is_ctx, (b * s + bi * c) // L + cj, bi * ncl + lj)

    def spec(col):
        return pl.BlockSpec((L, mix_w), lambda bi, di, j: (row_block(bi, di, j), col))

    return pl.pallas_call(
        _hgrn_kernel,
        grid=(b, 2, ncc + ncl),
        in_specs=[
            spec(0),
            pl.BlockSpec((L, mix_w), lambda bi, di, j: (row_block(bi, di, j), 1 + di)),
            spec(3),
            pl.BlockSpec((None, 3, mix_w), lambda bi, di, j: (di, 0, 0)),
            pl.BlockSpec((None,) + w_tab.shape[1:], lambda bi, di, j: (di, 0, 0)),
            pl.BlockSpec((None, L, L), lambda bi, di, j: (di, 0, 0)),
        ],
        out_specs=pl.BlockSpec((None, L, mix_w), lambda bi, di, j: (di, row_block(bi, di, j), 0)),
        out_shape=jax.ShapeDtypeStruct((2, n, mix_w), _F32),
        scratch_shapes=[pltpu.VMEM((nheads, HEAD_W, HEAD_W), _F32)],
        compiler_params=_params(("arbitrary", "arbitrary", "arbitrary")),
        name="hgrn_scan",
    )(p, p, p, lb_tab, w_tab, lmap)


def _attn_kernel(*refs, has_lat, tk, lam_init, q_scale):
    if has_lat:
        (q_ref, kc_ref, vc_ref, kl_ref, vl_ref, lamp_ref, nw_ref, o_ref, m_ref, acc_ref, vtc_ref, vtl_ref,
         st_ref, mx_ref) = refs
    else:
        q_ref, kc_ref, vc_ref, lamp_ref, nw_ref, o_ref, m_ref, acc_ref, vtc_ref = refs
    tq = q_ref.shape[0]
    eye = (lax.broadcasted_iota(jnp.int32, (HEAD_W, HEAD_W), 0)
           == lax.broadcasted_iota(jnp.int32, (HEAD_W, HEAD_W), 1)).astype(_F32).astype(_BF16)

    @pl.when(pl.program_id(2) == 0)
    def _():
        ones = jnp.ones((ATT_ONES_ROWS, vtc_ref.shape[1]), _BF16)
        vtc_ref[0:HEAD_W, :] = _dot_nt(eye, vc_ref[...]).astype(_BF16)
        vtc_ref[HEAD_W:, :] = ones
        if has_lat:
            def tbody(i, carry):
                off = pl.multiple_of(i * tk, tk)
                vtl_ref[0:HEAD_W, pl.ds(off, tk)] = _dot_nt(eye, vl_ref[pl.ds(off, tk), :]).astype(_BF16)
                vtl_ref[HEAD_W:, pl.ds(off, tk)] = jnp.ones((ATT_ONES_ROWS, tk), _BF16)
                return carry
            lax.fori_loop(0, vl_ref.shape[0] // tk, tbody, 0)

    q = q_ref[...].astype(_F32)
    if q_scale is not None:
        q = q * q_scale
    lane = lax.broadcasted_iota(jnp.int32, q.shape, 1)
    qq = jnp.concatenate([jnp.where(lane < DA_HEAD_DIM, q, 0.0), jnp.where(lane >= DA_HEAD_DIM, q, 0.0)],
                         axis=0).astype(_BF16)

    m_ref[...] = jnp.full_like(m_ref, -jnp.inf)
    acc_ref[...] = jnp.zeros_like(acc_ref)

    def softmax_pv(st, st_max, vt):
        m_prev = m_ref[...]
        m_new = jnp.maximum(m_prev, st_max)
        alpha = jnp.exp2(m_prev - m_new)
        pt = jnp.exp2(st - m_new).astype(_BF16)
        acc_ref[...] = alpha * acc_ref[...] + _dot(vt, pt)
        m_ref[...] = m_new

    def scores(t, buf):
        off = pl.multiple_of((t - 1) * tk, tk)
        st = _dot_nt(kl_ref[pl.ds(off, tk), :], qq)
        st_ref[buf] = st
        mx_ref[buf] = jnp.max(st, axis=0, keepdims=True)

    st_c = _dot_nt(kc_ref[...], qq)
    mx_c = jnp.max(st_c, axis=0, keepdims=True)
    if not has_lat:
        softmax_pv(st_c, mx_c, vtc_ref[...])
    else:
        n = kl_ref.shape[0] // tk

        def consume(t, buf):
            off = pl.multiple_of((t - 1) * tk, tk)
            softmax_pv(st_ref[buf], mx_ref[buf], vtl_ref[:, pl.ds(off, tk)])

        scores(1, 1)
        softmax_pv(st_c, mx_c, vtc_ref[...])
        n_pairs = (n - 1) // 2

        def body(j, carry):
            t = 2 * j + 1
            scores(t + 1, 0)
            consume(t, 1)
            scores(t + 2, 1)
            consume(t + 1, 0)
            return carry
        lax.fori_loop(0, n_pairs, body, 0)
        for t in range(2 * n_pairs + 1, n + 1):
            if t < n:
                scores(t + 1, (t + 1) % 2)
            consume(t, t % 2)

    lp = lamp_ref[...]
    lam = (jnp.exp(jnp.sum(lp[0:1] * lp[1:2], axis=-1, keepdims=True))
           - jnp.exp(jnp.sum(lp[2:3] * lp[3:4], axis=-1, keepdims=True)) + lam_init)
    on = acc_ref[0:HEAD_W, :] / acc_ref[HEAD_W:HEAD_W + 1, :]
    ot = on[:, 0:tq] - lam * on[:, tq:2 * tq]
    ms = jnp.mean(ot * ot, axis=0, keepdims=True)
    o = (ot * lax.rsqrt(ms + EPS)).T
    o_ref[...] = (o * nw_ref[...] * (1.0 - lam_init)).astype(o_ref.dtype)


def _attn(q_arr, q_col0, q_row0, tq, nq, p, kl_arr, b, s, c, mix_w, lam_p, nw, layer, lam_init, has_lat, tk=1024):
    nheads = mix_w // HEAD_W
    ak0, av0 = 9 * mix_w // HEAD_W, 10 * mix_w // HEAD_W
    ctx_blk0 = b * s // c
    in_specs = [
        pl.BlockSpec((tq, HEAD_W), lambda bi, h, qi: (q_row0 // tq + bi * nq + qi, q_col0 // HEAD_W + h)),
        pl.BlockSpec((c, HEAD_W), lambda bi, h, qi: (ctx_blk0 + bi, ak0 + h)),
        pl.BlockSpec((c, HEAD_W), lambda bi, h, qi: (ctx_blk0 + bi, av0 + h)),
    ]
    args = [q_arr, p, p]
    if has_lat:
        in_specs += [
            pl.BlockSpec((s, HEAD_W), lambda bi, h, qi: (bi, nheads + h)),
            pl.BlockSpec((s, HEAD_W), lambda bi, h, qi: (bi, av0 + h)),
        ]
        args += [kl_arr, p]
    in_specs += [
        pl.BlockSpec((None, 4, DA_HEAD_DIM), lambda bi, h, qi: (layer, 0, 0)),
        pl.BlockSpec((None, 1, HEAD_W), lambda bi, h, qi: (layer, 0, 0)),
    ]
    args += [lam_p, nw.reshape(-1, 1, HEAD_W)]
    vt_rows = HEAD_W + ATT_ONES_ROWS
    scratch = [pltpu.VMEM((1, 2 * tq), _F32), pltpu.VMEM((vt_rows, 2 * tq), _F32), pltpu.VMEM((vt_rows, c), _BF16)]
    tk = min(tk, s)
    if has_lat:
        scratch += [pltpu.VMEM((vt_rows, s), _BF16), pltpu.VMEM((2, tk, 2 * tq), _F32),
                    pltpu.VMEM((2, 1, 2 * tq), _F32)]
    return pl.pallas_call(
        functools.partial(_attn_kernel, has_lat=has_lat, tk=tk, lam_init=lam_init,
                          q_scale=None if has_lat else ATT_Q_SCALE),
        grid=(b, nheads, nq),
        in_specs=in_specs,
        out_specs=pl.BlockSpec((tq, HEAD_W), lambda bi, h, qi: (bi * nq + qi, h)),
        out_shape=jax.ShapeDtypeStruct((b * nq * tq, mix_w), _BF16),
        scratch_shapes=scratch,
        compiler_params=_params(("arbitrary", "arbitrary", "arbitrary")),
        name="diff_attn_lat" if has_lat else "diff_attn_ctx",
    )(*args)


def _branch_kernel(of_ref, ob_ref, hg_ref, cb_ref, cc_ref, cu_ref, ccp_ref, cup_ref, ccn_ref, cun_ref,
                   hnw_ref, cw_ref, a_ref, b_ref, *, n_lat, s, c):
    tm = of_ref.shape[0]
    nw = hnw_ref[...]
    for h in range(of_ref.shape[1] // HEAD_W):
        sl = slice(h * HEAD_W, (h + 1) * HEAD_W)
        o = of_ref[:, sl] + ob_ref[:, sl]
        ms = jnp.mean(o * o, axis=-1, keepdims=True)
        g = hg_ref[:, sl].astype(_F32)
        a_ref[:, sl] = ((o * lax.rsqrt(ms + EPS) * nw) * (g * jax.nn.sigmoid(g))).astype(a_ref.dtype)
    row = pl.program_id(0) * tm + lax.broadcasted_iota(jnp.int32, (tm, 1), 0)
    pos = jnp.where(row < n_lat, row % s, (row - n_lat) % c)
    seq_len = jnp.where(row < n_lat, s, c)
    local = lax.broadcasted_iota(jnp.int32, (tm, 1), 0)
    x = cc_ref[...].astype(_F32) * cu_ref[...].astype(_F32)
    x_halo_p = ccp_ref[7:8, :].astype(_F32) * cup_ref[7:8, :].astype(_F32)
    x_halo_n = ccn_ref[0:1, :].astype(_F32) * cun_ref[0:1, :].astype(_F32)
    x_prev = jnp.where(local == 0, x_halo_p, pltpu.roll(x, 1, axis=0))
    x_prev = jnp.where(pos == 0, 0.0, x_prev)
    x_next = jnp.where(local == tm - 1, x_halo_n, pltpu.roll(x, tm - 1, axis=0))
    x_next = jnp.where(pos == seq_len - 1, 0.0, x_next)
    y = cw_ref[0:1, :] * x_prev + cw_ref[1:2, :] * x + cw_ref[2:3, :] * x_next
    b_ref[...] = (cb_ref[...].astype(_F32) * y).astype(b_ref.dtype)


def _branches(o_fb, p, n_rows, hg_nw, conv_w, layer, b, s, c, mix_w, tm=256):
    tm = math.gcd(math.gcd(tm, s), b * c)
    last8 = p.shape[0] // 8 - 1
    main = lambda col: pl.BlockSpec((tm, mix_w), lambda i: (i, col))
    prev = lambda col: pl.BlockSpec((8, mix_w), lambda i: (jnp.maximum(i * (tm // 8) - 1, 0), col))
    nxt = lambda col: pl.BlockSpec((8, mix_w), lambda i: (jnp.minimum((i + 1) * (tm // 8), last8), col))
    out_sds = jax.ShapeDtypeStruct((n_rows, mix_w), _BF16)
    return pl.pallas_call(
        functools.partial(_branch_kernel, n_lat=b * s, s=s, c=c),
        grid=(n_rows // tm,),
        in_specs=[
            pl.BlockSpec((None, tm, mix_w), lambda i: (0, i, 0)),
            pl.BlockSpec((None, tm, mix_w), lambda i: (1, i, 0)),
            main(4), main(5), main(6), main(7), prev(6), prev(7), nxt(6), nxt(7),
            pl.BlockSpec((None, 1, HEAD_W), lambda i: (layer, 0, 0)),
            pl.BlockSpec((None, 3, mix_w), lambda i: (layer, 0, 0)),
        ],
        out_specs=[pl.BlockSpec((tm, mix_w), lambda i: (i, 0)), pl.BlockSpec((tm, mix_w), lambda i: (i, 0))],
        out_shape=[out_sds, out_sds],
        compiler_params=_params(("parallel",)),
        name="branch_prep",
    )(o_fb, o_fb, p, p, p, p, p, p, p, p, hg_nw.reshape(-1, 1, HEAD_W), conv_w)


def _merge_kernel(h_ref, mod_ref, a_ref, b_ref, c_ref, cx_ref, ga_ref, gb_ref, gc_ref, wbr_ref, wo_ref, o_ref, *,
                  n_lat_tiles):
    j = pl.program_id(1)
    tn = ga_ref.shape[1]

    @pl.when(j == 0)
    def _():
        o_ref[...] = jnp.zeros_like(o_ref)

    def gated(g_ref, x, k):
        return jax.nn.sigmoid(g_ref[...].astype(_F32)) * _dot(x, wbr_ref[k, j])

    c = jnp.where(pl.program_id(0) >= n_lat_tiles, cx_ref[...], c_ref[...])
    y = gated(ga_ref, a_ref[...], 0) + gated(gb_ref, b_ref[...], 1) + gated(gc_ref, c, 2)
    o_ref[...] += _dot(y.astype(_BF16), wo_ref[pl.ds(pl.multiple_of(j * tn, tn), tn), :])

    @pl.when(j == pl.num_programs(1) - 1)
    def _():
        o_ref[...] = h_ref[...] + mod_ref[5:6, :] * o_ref[...]


def _merge(h, n_rows, mod_l, a, bb, c_lat, c_ctx, p, w_branch, w_out, layer, dims, mix_w):
    d = h.shape[1]
    tn = w_branch.shape[4]
    tm = dims["tm"]
    mod_row = dims["mod_row"]
    g0 = 11 * mix_w // tn
    gstep = d // tn
    n_lat_tiles = c_lat.shape[0] // tm
    if c_ctx is None:
        c_ctx = c_lat
    br = lambda: pl.BlockSpec((tm, mix_w), lambda i, j: (i, 0))
    gate = lambda k: pl.BlockSpec((tm, tn), lambda i, j: (i, g0 + k * gstep + j))
    return pl.pallas_call(
        functools.partial(_merge_kernel, n_lat_tiles=n_lat_tiles),
        grid=(n_rows // tm, d // tn),
        in_specs=[
            pl.BlockSpec((tm, d), lambda i, j: (i, 0)),
            pl.BlockSpec((None, N_MOD, d), lambda i, j: (mod_row(i), 0, 0)),
            br(), br(),
            pl.BlockSpec((tm, mix_w), lambda i, j: (jnp.minimum(i, n_lat_tiles - 1), 0)),
            pl.BlockSpec((tm, mix_w), lambda i, j: (jnp.maximum(i - n_lat_tiles, 0), 0)),
            gate(0), gate(1), gate(2),
            pl.BlockSpec((None,) + w_branch.shape[1:], lambda i, j: (layer, 0, 0, 0, 0),
                         pipeline_mode=pl.Buffered(1)),
            pl.BlockSpec((None, d, d), lambda i, j: (layer, 0, 0), pipeline_mode=pl.Buffered(1)),
        ],
        out_specs=pl.BlockSpec((tm, d), lambda i, j: (i, 0)),
        out_shape=jax.ShapeDtypeStruct((n_rows, d), _F32),
        compiler_params=_params(("parallel", "arbitrary")),
        name="merge_out",
    )(h, mod_l, a, bb, c_lat, c_ctx, p, p, p, w_branch, w_out)


def kernel(x, c, ctx, c_ctx, w_ada, b_ada, norm_w, ffn_w13, ffn_w2, w_in, hgrn_lb, hgrn_norm_w, conv_w,
           da_lambda, da_norm_w, w_branch, w_out, final_norm_w):
    b, s, d = x.shape
    cl = ctx.shape[1]
    depth = w_ada.shape[0]
    mix_w = d // 2
    n_lat, n_ctx = b * s, b * cl
    n_all = n_lat + n_ctx
    tm = math.gcd(math.gcd(512, s), n_ctx)
    dims = {
        "tm": tm,
        "tm_big": math.gcd(1024, s),
        "s": s,
        "b": b,
        "mod_row": lambda i: jnp.minimum(i * tm // s, b),
        "norm_w": norm_w.reshape(depth * 3, 1, d),
    }

    w13_b, w2_b = ffn_w13.astype(_BF16), ffn_w2.astype(_BF16)
    w_in_b, w_out_b = w_in.astype(_BF16), w_out.astype(_BF16)
    w_branch_b = (w_branch.astype(_BF16).reshape(depth, w_branch.shape[1], mix_w, d // MERGE_TN, MERGE_TN)
                  .transpose(0, 1, 3, 2, 4))

    cvec = jnp.concatenate([c, c_ctx[None, :], jnp.zeros((8 - b - 1, d), _F32)], axis=0)
    mod = _ada_mod(cvec, w_ada, b_ada).reshape(depth, 8, N_MOD, d)

    cum = jnp.cumsum(jax.nn.softmax(hgrn_lb.astype(_F32), axis=1), axis=1)
    lb = cum - cum[:, :1]
    lb_tab = jnp.stack([jnp.log(lb), jnp.log1p(-lb), 1.0 - lb], axis=2)
    w_tab, lmap = _hgrn_tables()
    cos, sin = _rope_tables(s)

    h = jnp.concatenate([x.reshape(n_lat, d), ctx.reshape(n_ctx, d)], axis=0)
    for layer in range(depth):
        last = layer == depth - 1
        lam_init = 0.8 - 0.6 * math.exp(-0.3 * layer)
        mod_l = mod[layer]
        h = _ffn(h, n_all, mod_l, layer * 3, w13_b, w2_b, layer, 0, 0, dims)
        p = _mixer_in(h, mod_l, layer * 3 + 1, w_in_b, layer, dims)
        qk = _rope(p, cos, sin, b, s, 8 * mix_w)
        o_fb = _hgrn(p, lb_tab[:, layer], w_tab, lmap, b, s, cl, mix_w)
        tq = min(512, s)
        c_att = _attn(qk, 0, 0, tq, s // tq, p, qk, b, s, cl, mix_w, da_lambda, da_norm_w, layer, lam_init, True)
        n_rows = n_lat if last else n_all
        c_ctx_att = None
        if not last:
            c_ctx_att = _attn(p, 8 * mix_w, n_lat, cl, 1, p, None, b, s, cl, mix_w, da_lambda, da_norm_w,
                              layer, lam_init, False)
        a_br, b_br = _branches(o_fb, p, n_rows, hgrn_norm_w, conv_w, layer, b, s, cl, mix_w)
        h = _merge(h, n_rows, mod_l, a_br, b_br, c_att, c_ctx_att, p, w_branch_b, w_out_b, layer, dims, mix_w)
        h = _ffn(h, n_rows, mod_l, layer * 3 + 2, w13_b, w2_b, layer, 1, 6, dims,
                 final_w=final_norm_w if last else None)
    return h.reshape(b, s, d)
```

```python
import functools
import math

import jax
import jax.numpy as jnp
import numpy as np
from jax import lax
from jax.experimental import pallas as pl
from jax.experimental.pallas import tpu as pltpu

_F32 = jnp.float32
_BF16 = jnp.bfloat16

EPS = 1e-6
N_MOD = 9
GRID_W = 64
ROPE_THETA = 10000.0
HEAD_W = 128
DA_HEAD_DIM = 64
LOG2_E = math.log2(math.e)
ATT_Q_SCALE = LOG2_E * DA_HEAD_DIM ** -0.5
ATT_ONES_ROWS = 16
HG_CHUNK = 64
HG_LEVELS = 6
FFN_TF = 512
MERGE_TN = 512
ROW_CHUNK = 128
V7X_VMEM_LIMIT = 56 * 1024 * 1024
V7X_VMEM_LIMIT_FFN = 60 * 1024 * 1024


def _dot(a, b):
    return jnp.dot(a, b, preferred_element_type=_F32)


def _dot_nt(a, b):
    return lax.dot_general(a, b, (((1,), (1,)), ((), ())), preferred_element_type=_F32)


def _dot_tn(a, b):
    return lax.dot_general(a, b, (((0,), (0,)), ((), ())), preferred_element_type=_F32)


def _params(semantics, vmem=V7X_VMEM_LIMIT):
    return pltpu.CompilerParams(dimension_semantics=semantics, vmem_limit_bytes=vmem)


def _for_row_chunks(n_rows, body):
    def step(c, carry):
        body(pl.multiple_of(c * ROW_CHUNK, ROW_CHUNK))
        return carry
    lax.fori_loop(0, n_rows // ROW_CHUNK, step, 0)


def _norm_mod(h, nw, shift, scale):
    ms = jnp.mean(h * h, axis=-1, keepdims=True)
    return (h * lax.rsqrt(ms + EPS) * nw) * (1.0 + scale) + shift


def _ada_kernel(c_ref, w_ref, b_ref, o_ref):
    c = c_ref[...]
    s = (c * jax.nn.sigmoid(c)).astype(_BF16)
    o_ref[...] = _dot(s, w_ref[...].astype(_BF16)) + b_ref[...]


def _ada_mod(cvec, w_ada, b_ada, tn=1024):
    depth, d, nd = w_ada.shape
    rows = cvec.shape[0]
    return pl.pallas_call(
        _ada_kernel,
        grid=(depth, nd // tn),
        in_specs=[
            pl.BlockSpec((rows, d), lambda l, j: (0, 0)),
            pl.BlockSpec((None, d, tn), lambda l, j: (l, 0, j)),
            pl.BlockSpec((None, 1, tn), lambda l, j: (l, 0, j)),
        ],
        out_specs=pl.BlockSpec((None, rows, tn), lambda l, j: (l, 0, j)),
        out_shape=jax.ShapeDtypeStruct((depth, rows, nd), _F32),
        compiler_params=_params(("arbitrary", "arbitrary")),
        name="ada_mod",
    )(cvec, w_ada, b_ada.reshape(depth, 1, nd))


def _on_row_tiles(tail_rows, tm, body):
    if tail_rows == 0:
        body(tm)
        return
    is_tail = pl.program_id(0) == pl.num_programs(0) - 1

    @pl.when(jnp.logical_not(is_tail))
    def _():
        body(tm)

    @pl.when(is_tail)
    def _():
        body(tail_rows)


def _ffn_kernel(h_ref, mod_ref, nw_ref, w1_ref, w3_ref, w2_ref, *rest, mod_off, final_norm, tail_rows):
    if final_norm:
        fw_ref, o_ref, xn_ref = rest
    else:
        o_ref, xn_ref = rest
    j = pl.program_id(1)

    def body(n_rows):
        @pl.when(j == 0)
        def _():
            def rows(r0):
                rs = pl.ds(r0, ROW_CHUNK)
                xn = _norm_mod(h_ref[rs, :], nw_ref[...], mod_ref[mod_off:mod_off + 1, :],
                               mod_ref[mod_off + 1:mod_off + 2, :])
                xn_ref[rs, :] = xn.astype(_BF16)
                o_ref[rs, :] = jnp.zeros((ROW_CHUNK, o_ref.shape[1]), o_ref.dtype)
            _for_row_chunks(n_rows, rows)

        xn = xn_ref[0:n_rows, :]
        g = _dot(xn, w1_ref[...])
        u = _dot(xn, w3_ref[...])
        a = (g * jax.nn.sigmoid(g) * u).astype(_BF16)
        o_ref[0:n_rows, :] += _dot(a, w2_ref[...])

        @pl.when(j == pl.num_programs(1) - 1)
        def _():
            gate = mod_ref[mod_off + 2:mod_off + 3, :]

            def rows(r0):
                rs = pl.ds(r0, ROW_CHUNK)
                out = h_ref[rs, :] + 0.5 * gate * o_ref[rs, :]
                if final_norm:
                    ms = jnp.mean(out * out, axis=-1, keepdims=True)
                    out = out * lax.rsqrt(ms + EPS) * fw_ref[...]
                o_ref[rs, :] = out
            _for_row_chunks(n_rows, rows)

    _on_row_tiles(tail_rows, h_ref.shape[0], body)


def _ffn(h, n_rows, mod_l, nw, w13, w2, layer, which, mod_off, dims, final_w=None, tf=FFN_TF):
    d = h.shape[1]
    nfb = w2.shape[2] // tf
    tm = dims["tm_big"]
    mod_row = lambda i: jnp.minimum(i * tm // dims["s"], dims["b"])
    in_specs = [
        pl.BlockSpec((tm, d), lambda i, j: (i, 0)),
        pl.BlockSpec((None, N_MOD, d), lambda i, j: (mod_row(i), 0, 0)),
        pl.BlockSpec((None, 1, d), lambda i, j: (nw, 0, 0)),
        pl.BlockSpec((None, None, d, tf), lambda i, j: (layer, which, 0, j)),
        pl.BlockSpec((None, None, d, tf), lambda i, j: (layer, which, 0, j + nfb)),
        pl.BlockSpec((None, None, tf, d), lambda i, j: (layer, which, j, 0)),
    ]
    args = [h, mod_l, dims["norm_w"], w13, w13, w2]
    if final_w is not None:
        in_specs.append(pl.BlockSpec((1, d), lambda i, j: (0, 0)))
        args.append(final_w.reshape(1, d))
    return pl.pallas_call(
        functools.partial(_ffn_kernel, mod_off=mod_off, final_norm=final_w is not None, tail_rows=n_rows % tm),
        grid=(pl.cdiv(n_rows, tm), nfb),
        in_specs=in_specs,
        out_specs=pl.BlockSpec((tm, d), lambda i, j: (i, 0)),
        out_shape=jax.ShapeDtypeStruct((n_rows, d), _F32),
        scratch_shapes=[pltpu.VMEM((tm, d), _BF16)],
        compiler_params=_params(("parallel", "arbitrary"), vmem=V7X_VMEM_LIMIT_FFN),
        name="half_ffn",
    )(*args)


def _proj_kernel(h_ref, mod_ref, nw_ref, w_ref, o_ref, xn_ref, *, tail_rows):
    def body(n_rows):
        @pl.when(pl.program_id(1) == 0)
        def _():
            xn = _norm_mod(h_ref[0:n_rows, :], nw_ref[...], mod_ref[3:4, :], mod_ref[4:5, :])
            xn_ref[0:n_rows, :] = xn.astype(_BF16)

        o_ref[0:n_rows, :] = _dot(xn_ref[0:n_rows, :], w_ref[...]).astype(o_ref.dtype)

    _on_row_tiles(tail_rows, h_ref.shape[0], body)


def _mixer_in(h, mod_l, nw, w_in, layer, dims, tn=1024):
    n, d = h.shape
    cols = w_in.shape[2]
    tm = dims["tm_big"]
    mod_row = lambda i: jnp.minimum(i * tm // dims["s"], dims["b"])
    return pl.pallas_call(
        functools.partial(_proj_kernel, tail_rows=n % tm),
        grid=(pl.cdiv(n, tm), cols // tn),
        in_specs=[
            pl.BlockSpec((tm, d), lambda i, j: (i, 0)),
            pl.BlockSpec((None, N_MOD, d), lambda i, j: (mod_row(i), 0, 0)),
            pl.BlockSpec((None, 1, d), lambda i, j: (nw, 0, 0)),
            pl.BlockSpec((None, d, tn), lambda i, j: (layer, 0, j)),
        ],
        out_specs=pl.BlockSpec((tm, tn), lambda i, j: (i, j)),
        out_shape=jax.ShapeDtypeStruct((n, cols), _BF16),
        scratch_shapes=[pltpu.VMEM((tm, d), _BF16)],
        compiler_params=_params(("parallel", "arbitrary")),
        name="mixer_in",
    )(h, mod_l, dims["norm_w"], w_in)


def _rope_kernel(x_ref, cos_ref, sin_ref, o_ref):
    cos = cos_ref[...]
    sin = sin_ref[...]
    lane = lax.broadcasted_iota(jnp.int32, cos.shape, 1)
    first = (lane % 32) < 16
    n_groups = x_ref.shape[1] // HEAD_W
    for g in range(n_groups):
        sl = slice(g * HEAD_W, (g + 1) * HEAD_W)
        x = x_ref[:, sl].astype(_F32)
        rot = jnp.where(first, pltpu.roll(x, HEAD_W - 16, axis=1), pltpu.roll(x, 16, axis=1))
        y = x * cos + rot * sin
        if g < n_groups // 2:
            y = y * ATT_Q_SCALE
        o_ref[:, sl] = y.astype(o_ref.dtype)


def _rope_tables(n_tokens):
    rows = n_tokens // GRID_W
    row = jnp.repeat(jnp.arange(rows, dtype=jnp.int32), GRID_W)
    col = jnp.tile(jnp.arange(GRID_W, dtype=jnp.int32), rows)
    half = DA_HEAD_DIM // 2
    inv_freq = ROPE_THETA ** (-jnp.arange(0, half, 2, dtype=_F32) / half)
    ang_r = row.astype(_F32)[:, None] * inv_freq
    ang_c = col.astype(_F32)[:, None] * inv_freq
    ang = jnp.concatenate([ang_r, ang_r, ang_c, ang_c], axis=-1)
    ang = jnp.concatenate([ang, ang], axis=-1)
    sign = jnp.where((jnp.arange(HEAD_W) % 32) < 16, -1.0, 1.0)
    return jnp.cos(ang), jnp.sin(ang) * sign


def _rope(p, cos, sin, b, s, col0, tr=512):
    width = 2 * dims_mix_w(p)
    tr = min(tr, s)
    return pl.pallas_call(
        _rope_kernel,
        grid=(b * s // tr,),
        in_specs=[
            pl.BlockSpec((tr, width), lambda i: (i, col0 // width)),
            pl.BlockSpec((tr, HEAD_W), lambda i: (i % (s // tr), 0)),
            pl.BlockSpec((tr, HEAD_W), lambda i: (i % (s // tr), 0)),
        ],
        out_specs=pl.BlockSpec((tr, width), lambda i: (i, 0)),
        out_shape=jax.ShapeDtypeStruct((b * s, width), _BF16),
        compiler_params=_params(("parallel",)),
        name="axial_rope",
    )(p, cos, sin)


def dims_mix_w(p):
    return p.shape[1] // 17


def _hgrn_tables():
    L = HG_CHUNK
    w = np.zeros((2, (HG_LEVELS + 2) * L + 8, L), np.float32)
    lmap = np.full((2, L, L), HG_LEVELS + 1, np.int32)
    r = np.arange(L)
    for t in range(L):
        w[0, t, r <= t] = 1
        w[0, L + t, r > t] = 1
        w[1, t, r >= t] = 1
        w[1, L + t, r < t] = 1
        lmap[:, t, t] = 0
        for lvl in range(1, HG_LEVELS + 1):
            bs = (2 * L) >> lvl
            mid = (t // bs) * bs + bs // 2
            base = (lvl + 1) * L + t
            if t >= mid:
                w[0, base, (r >= mid) & (r <= t)] = 1
                w[1, base, (r >= mid) & (r < t)] = 1
            else:
                w[0, base, (r > t) & (r < mid)] = 1
                w[1, base, (r >= t) & (r < mid)] = 1
            for s_ in range(L):
                if s_ // bs == t // bs and t >= mid and s_ < mid:
                    lmap[0, t, s_] = lvl
                    lmap[1, s_, t] = lvl
    w[:, (HG_LEVELS + 2) * L:, :] = 1
    w = np.concatenate([w, w], axis=2)
    return jnp.asarray(w, _BF16), jnp.asarray(lmap)


def _hgrn_kernel(q_ref, z_ref, v_ref, lb_ref, w_ref, lmap_ref, o_ref, st_ref):
    L = HG_CHUNK

    @pl.when(pl.program_id(2) == 0)
    def _():
        st_ref[...] = jnp.zeros_like(st_ref)

    heads = [slice(h * HEAD_W, (h + 1) * HEAD_W) for h in range(q_ref.shape[1] // HEAD_W)]
    lmap = lmap_ref[...]
    z = z_ref[...].astype(_F32)
    ls = jnp.minimum(z, 0.0) - jnp.log(1.0 + jnp.exp(-jnp.abs(z)))
    bb = lb_ref[1:2, :] + ls
    log_lb = lb_ref[0:1, :]
    lf = jnp.maximum(log_lb, bb) + jnp.log(1.0 + jnp.exp(-jnp.abs(log_lb - bb)))
    k = lb_ref[2:3, :] * jnp.exp(ls - z)
    lf2 = lf * LOG2_E
    lf_hi = lf2.astype(_BF16)
    lf_lo = (lf2 - lf_hi.astype(_F32)).astype(_BF16)
    e = jnp.exp2(_dot(w_ref[...], jnp.concatenate([lf_hi, lf_lo], axis=0)))
    q = q_ref[...].astype(_F32)
    v = v_ref[...]
    qk = q * k
    a = [jnp.where(lmap == 0, jnp.sum(qk[:, sl], axis=-1, keepdims=True) * jnp.ones((1, L), _F32), 0.0)
         for sl in heads]
    for lvl in range(1, HG_LEVELS + 1):
        el = e[(lvl + 1) * L:(lvl + 2) * L]
        ql = (q * el).astype(_BF16)
        kl = (k * el).astype(_BF16)
        a = [jnp.where(lmap == lvl, _dot_nt(ql[:, sl], kl[:, sl]), a[h]) for h, sl in enumerate(heads)]
    q_in = (q * e[0:L]).astype(_BF16)
    k_out = (k * e[L:2 * L]).astype(_BF16)
    tot = e[(HG_LEVELS + 2) * L:(HG_LEVELS + 2) * L + 1]
    for h, sl in enumerate(heads):
        st = st_ref[h]
        o_ref[:, sl] = _dot(a[h].astype(_BF16), v[:, sl]) + _dot_nt(q_in[:, sl], st.astype(_BF16))
        st_ref[h] = tot[:, sl] * st + _dot_tn(v[:, sl], k_out[:, sl])


def _hgrn(p, lb_tab, w_tab, lmap, b, s, c, mix_w):
    n = p.shape[0]
    L = HG_CHUNK
    ncc, ncl = c // L, s // L
    nheads = mix_w // HEAD_W

    def row_block(bi, di, j):
        is_ctx = j < ncc
        cj = jnp.where(di == 0, j, ncc - 1 - j)
        lj = jnp.where(di == 0, j - ncc, ncl - 1 - (j - ncc))
        return jnp.where(is_ctx, (b * s + bi * c) // L + cj, bi * ncl + lj)

    def spec(col):
        return pl.BlockSpec((L, mix_w), lambda bi, di, j: (row_block(bi, di, j), col))

    return pl.pallas_call(
        _hgrn_kernel,
        grid=(b, 2, ncc + ncl),
        in_specs=[
            spec(0),
            pl.BlockSpec((L, mix_w), lambda bi, di, j: (row_block(bi, di, j), 1 + di)),
            spec(3),
            pl.BlockSpec((None, 3, mix_w), lambda bi, di, j: (di, 0, 0)),
            pl.BlockSpec((None,) + w_tab.shape[1:], lambda bi, di, j: (di, 0, 0)),
            pl.BlockSpec((None, L, L), lambda bi, di, j: (di, 0, 0)),
        ],
        out_specs=pl.BlockSpec((None, L, mix_w), lambda bi, di, j: (di, row_block(bi, di, j), 0)),
        out_shape=jax.ShapeDtypeStruct((2, n, mix_w), _F32),
        scratch_shapes=[pltpu.VMEM((nheads, HEAD_W, HEAD_W), _F32)],
        compiler_params=_params(("arbitrary", "arbitrary", "arbitrary")),
        name="hgrn_scan",
    )(p, p, p, lb_tab, w_tab, lmap)


def _attn_kernel(*refs, has_lat, tk, lam_init, q_scale):
    if has_lat:
        (q_ref, kc_ref, vc_ref, kl_ref, vl_ref, lamp_ref, nw_ref, o_ref, m_ref, acc_ref, vtc_ref, vtl_ref,
         st_ref, mx_ref) = refs
    else:
        q_ref, kc_ref, vc_ref, lamp_ref, nw_ref, o_ref, m_ref, acc_ref, vtc_ref = refs
    tq = q_ref.shape[0]
    eye = (lax.broadcasted_iota(jnp.int32, (HEAD_W, HEAD_W), 0)
           == lax.broadcasted_iota(jnp.int32, (HEAD_W, HEAD_W), 1)).astype(_F32).astype(_BF16)

    @pl.when(pl.program_id(2) == 0)
    def _():
        ones = jnp.ones((ATT_ONES_ROWS, vtc_ref.shape[1]), _BF16)
        vtc_ref[0:HEAD_W, :] = _dot_nt(eye, vc_ref[...]).astype(_BF16)
        vtc_ref[HEAD_W:, :] = ones
        if has_lat:
            def tbody(i, carry):
                off = pl.multiple_of(i * tk, tk)
                vtl_ref[0:HEAD_W, pl.ds(off, tk)] = _dot_nt(eye, vl_ref[pl.ds(off, tk), :]).astype(_BF16)
                vtl_ref[HEAD_W:, pl.ds(off, tk)] = jnp.ones((ATT_ONES_ROWS, tk), _BF16)
                return carry
            lax.fori_loop(0, vl_ref.shape[0] // tk, tbody, 0)

    q = q_ref[...].astype(_F32)
    if q_scale is not None:
        q = q * q_scale
    lane = lax.broadcasted_iota(jnp.int32, q.shape, 1)
    qq = jnp.concatenate([jnp.where(lane < DA_HEAD_DIM, q, 0.0), jnp.where(lane >= DA_HEAD_DIM, q, 0.0)],
                         axis=0).astype(_BF16)

    m_ref[...] = jnp.full_like(m_ref, -jnp.inf)
    acc_ref[...] = jnp.zeros_like(acc_ref)

    def softmax_pv(st, st_max, vt):
        m_prev = m_ref[...]
        m_new = jnp.maximum(m_prev, st_max)
        alpha = jnp.exp2(m_prev - m_new)
        pt = jnp.exp2(st - m_new).astype(_BF16)
        acc_ref[...] = alpha * acc_ref[...] + _dot(vt, pt)
        m_ref[...] = m_new

    def scores(t, buf):
        off = pl.multiple_of((t - 1) * tk, tk)
        st = _dot_nt(kl_ref[pl.ds(off, tk), :], qq)
        st_ref[buf] = st
        mx_ref[buf] = jnp.max(st, axis=0, keepdims=True)

    st_c = _dot_nt(kc_ref[...], qq)
    mx_c = jnp.max(st_c, axis=0, keepdims=True)
    if not has_lat:
        softmax_pv(st_c, mx_c, vtc_ref[...])
    else:
        n = kl_ref.shape[0] // tk

        def consume(t, buf):
            off = pl.multiple_of((t - 1) * tk, tk)
            softmax_pv(st_ref[buf], mx_ref[buf], vtl_ref[:, pl.ds(off, tk)])

        scores(1, 1)
        softmax_pv(st_c, mx_c, vtc_ref[...])
        n_pairs = (n - 1) // 2

        def body(j, carry):
            t = 2 * j + 1
            scores(t + 1, 0)
            consume(t, 1)
            scores(t + 2, 1)
            consume(t + 1, 0)
            return carry
        lax.fori_loop(0, n_pairs, body, 0)
        for t in range(2 * n_pairs + 1, n + 1):
            if t < n:
                scores(t + 1, (t + 1) % 2)
            consume(t, t % 2)

    lp = lamp_ref[...]
    lam = (jnp.exp(jnp.sum(lp[0:1] * lp[1:2], axis=-1, keepdims=True))
           - jnp.exp(jnp.sum(lp[2:3] * lp[3:4], axis=-1, keepdims=True)) + lam_init)
    on = acc_ref[0:HEAD_W, :] / acc_ref[HEAD_W:HEAD_W + 1, :]
    ot = on[:, 0:tq] - lam * on[:, tq:2 * tq]
    ms = jnp.mean(ot * ot, axis=0, keepdims=True)
    o = (ot * lax.rsqrt(ms + EPS)).T
    o_ref[...] = (o * nw_ref[...] * (1.0 - lam_init)).astype(o_ref.dtype)


def _attn(q_arr, q_col0, q_row0, tq, nq, p, kl_arr, b, s, c, mix_w, lam_p, nw, layer, lam_init, has_lat, tk=1024):
    nheads = mix_w // HEAD_W
    ak0, av0 = 9 * mix_w // HEAD_W, 10 * mix_w // HEAD_W
    ctx_blk0 = b * s // c
    in_specs = [
        pl.BlockSpec((tq, HEAD_W), lambda bi, h, qi: (q_row0 // tq + bi * nq + qi, q_col0 // HEAD_W + h)),
        pl.BlockSpec((c, HEAD_W), lambda bi, h, qi: (ctx_blk0 + bi, ak0 + h)),
        pl.BlockSpec((c, HEAD_W), lambda bi, h, qi: (ctx_blk0 + bi, av0 + h)),
    ]
    args = [q_arr, p, p]
    if has_lat:
        in_specs += [
            pl.BlockSpec((s, HEAD_W), lambda bi, h, qi: (bi, nheads + h)),
            pl.BlockSpec((s, HEAD_W), lambda bi, h, qi: (bi, av0 + h)),
        ]
        args += [kl_arr, p]
    in_specs += [
        pl.BlockSpec((None, 4, DA_HEAD_DIM), lambda bi, h, qi: (layer, 0, 0)),
        pl.BlockSpec((None, 1, HEAD_W), lambda bi, h, qi: (layer, 0, 0)),
    ]
    args += [lam_p, nw.reshape(-1, 1, HEAD_W)]
    vt_rows = HEAD_W + ATT_ONES_ROWS
    scratch = [pltpu.VMEM((1, 2 * tq), _F32), pltpu.VMEM((vt_rows, 2 * tq), _F32), pltpu.VMEM((vt_rows, c), _BF16)]
    tk = min(tk, s)
    if has_lat:
        scratch += [pltpu.VMEM((vt_rows, s), _BF16), pltpu.VMEM((2, tk, 2 * tq), _F32),
                    pltpu.VMEM((2, 1, 2 * tq), _F32)]
    return pl.pallas_call(
        functools.partial(_attn_kernel, has_lat=has_lat, tk=tk, lam_init=lam_init,
                          q_scale=None if has_lat else ATT_Q_SCALE),
        grid=(b, nheads, nq),
        in_specs=in_specs,
        out_specs=pl.BlockSpec((tq, HEAD_W), lambda bi, h, qi: (bi * nq + qi, h)),
        out_shape=jax.ShapeDtypeStruct((b * nq * tq, mix_w), _BF16),
        scratch_shapes=scratch,
        compiler_params=_params(("arbitrary", "arbitrary", "arbitrary")),
        name="diff_attn_lat" if has_lat else "diff_attn_ctx",
    )(*args)


def _branch_kernel(of_ref, ob_ref, hg_ref, cb_ref, cc_ref, cu_ref, ccp_ref, cup_ref, ccn_ref, cun_ref,
                   hnw_ref, cw_ref, a_ref, b_ref, *, n_lat, s, c):
    tm = of_ref.shape[0]
    nw = hnw_ref[...]
    for h in range(of_ref.shape[1] // HEAD_W):
        sl = slice(h * HEAD_W, (h + 1) * HEAD_W)
        o = of_ref[:, sl] + ob_ref[:, sl]
        ms = jnp.mean(o * o, axis=-1, keepdims=True)
        g = hg_ref[:, sl].astype(_F32)
        a_ref[:, sl] = ((o * lax.rsqrt(ms + EPS) * nw) * (g * jax.nn.sigmoid(g))).astype(a_ref.dtype)
    row = pl.program_id(0) * tm + lax.broadcasted_iota(jnp.int32, (tm, 1), 0)
    pos = jnp.where(row < n_lat, row % s, (row - n_lat) % c)
    seq_len = jnp.where(row < n_lat, s, c)
    local = lax.broadcasted_iota(jnp.int32, (tm, 1), 0)
    first, last = pos == 0, pos == seq_len - 1
    for g in range(cc_ref.shape[1] // HEAD_W):
        sl = slice(g * HEAD_W, (g + 1) * HEAD_W)
        x = cc_ref[:, sl].astype(_F32) * cu_ref[:, sl].astype(_F32)
        x_halo_p = ccp_ref[7:8, sl].astype(_F32) * cup_ref[7:8, sl].astype(_F32)
        x_halo_n = ccn_ref[0:1, sl].astype(_F32) * cun_ref[0:1, sl].astype(_F32)
        x_prev = jnp.where(local == 0, x_halo_p, pltpu.roll(x, 1, axis=0))
        x_prev = jnp.where(first, 0.0, x_prev)
        x_next = jnp.where(local == tm - 1, x_halo_n, pltpu.roll(x, tm - 1, axis=0))
        x_next = jnp.where(last, 0.0, x_next)
        y = cw_ref[0:1, sl] * x_prev + cw_ref[1:2, sl] * x + cw_ref[2:3, sl] * x_next
        b_ref[:, sl] = (cb_ref[:, sl].astype(_F32) * y).astype(b_ref.dtype)


def _branches(o_fb, p, n_rows, hg_nw, conv_w, layer, b, s, c, mix_w, tm=256):
    tm = math.gcd(math.gcd(tm, s), b * c)
    last8 = p.shape[0] // 8 - 1
    main = lambda col: pl.BlockSpec((tm, mix_w), lambda i: (i, col))
    prev = lambda col: pl.BlockSpec((8, mix_w), lambda i: (jnp.maximum(i * (tm // 8) - 1, 0), col))
    nxt = lambda col: pl.BlockSpec((8, mix_w), lambda i: (jnp.minimum((i + 1) * (tm // 8), last8), col))
    out_sds = jax.ShapeDtypeStruct((n_rows, mix_w), _BF16)
    return pl.pallas_call(
        functools.partial(_branch_kernel, n_lat=b * s, s=s, c=c),
        grid=(n_rows // tm,),
        in_specs=[
            pl.BlockSpec((None, tm, mix_w), lambda i: (0, i, 0)),
            pl.BlockSpec((None, tm, mix_w), lambda i: (1, i, 0)),
            main(4), main(5), main(6), main(7), prev(6), prev(7), nxt(6), nxt(7),
            pl.BlockSpec((None, 1, HEAD_W), lambda i: (layer, 0, 0)),
            pl.BlockSpec((None, 3, mix_w), lambda i: (layer, 0, 0)),
        ],
        out_specs=[pl.BlockSpec((tm, mix_w), lambda i: (i, 0)), pl.BlockSpec((tm, mix_w), lambda i: (i, 0))],
        out_shape=[out_sds, out_sds],
        compiler_params=_params(("parallel",)),
        name="branch_prep",
    )(o_fb, o_fb, p, p, p, p, p, p, p, p, hg_nw.reshape(-1, 1, HEAD_W), conv_w)


def _merge_kernel(h_ref, mod_ref, a_ref, b_ref, c_ref, cx_ref, ga_ref, gb_ref, gc_ref, wbr_ref, wo_ref, o_ref, *,
                  n_lat_tiles):
    j = pl.program_id(1)
    tn = ga_ref.shape[1]

    @pl.when(j == 0)
    def _():
        o_ref[...] = jnp.zeros_like(o_ref)

    def gated(g_ref, x, k):
        return jax.nn.sigmoid(g_ref[...].astype(_F32)) * _dot(x, wbr_ref[k, j])

    c = jnp.where(pl.program_id(0) >= n_lat_tiles, cx_ref[...], c_ref[...])
    y = gated(ga_ref, a_ref[...], 0) + gated(gb_ref, b_ref[...], 1) + gated(gc_ref, c, 2)
    o_ref[...] += _dot(y.astype(_BF16), wo_ref[pl.ds(pl.multiple_of(j * tn, tn), tn), :])

    @pl.when(j == pl.num_programs(1) - 1)
    def _():
        o_ref[...] = h_ref[...] + mod_ref[5:6, :] * o_ref[...]


def _merge(h, n_rows, mod_l, a, bb, c_lat, c_ctx, p, w_branch, w_out, layer, dims, mix_w):
    d = h.shape[1]
    tn = w_branch.shape[4]
    tm = dims["tm"]
    mod_row = dims["mod_row"]
    g0 = 11 * mix_w // tn
    gstep = d // tn
    n_lat_tiles = c_lat.shape[0] // tm
    if c_ctx is None:
        c_ctx = c_lat
    br = lambda: pl.BlockSpec((tm, mix_w), lambda i, j: (i, 0))
    gate = lambda k: pl.BlockSpec((tm, tn), lambda i, j: (i, g0 + k * gstep + j))
    return pl.pallas_call(
        functools.partial(_merge_kernel, n_lat_tiles=n_lat_tiles),
        grid=(n_rows // tm, d // tn),
        in_specs=[
            pl.BlockSpec((tm, d), lambda i, j: (i, 0)),
            pl.BlockSpec((None, N_MOD, d), lambda i, j: (mod_row(i), 0, 0)),
            br(), br(),
            pl.BlockSpec((tm, mix_w), lambda i, j: (jnp.minimum(i, n_lat_tiles - 1), 0)),
            pl.BlockSpec((tm, mix_w), lambda i, j: (jnp.maximum(i - n_lat_tiles, 0), 0)),
            gate(0), gate(1), gate(2),
            pl.BlockSpec((None,) + w_branch.shape[1:], lambda i, j: (layer, 0, 0, 0, 0),
                         pipeline_mode=pl.Buffered(1)),
            pl.BlockSpec((None, d, d), lambda i, j: (layer, 0, 0), pipeline_mode=pl.Buffered(1)),
        ],
        out_specs=pl.BlockSpec((tm, d), lambda i, j: (i, 0)),
        out_shape=jax.ShapeDtypeStruct((n_rows, d), _F32),
        compiler_params=_params(("parallel", "arbitrary")),
        name="merge_out",
    )(h, mod_l, a, bb, c_lat, c_ctx, p, p, p, w_branch, w_out)


def kernel(x, c, ctx, c_ctx, w_ada, b_ada, norm_w, ffn_w13, ffn_w2, w_in, hgrn_lb, hgrn_norm_w, conv_w,
           da_lambda, da_norm_w, w_branch, w_out, final_norm_w):
    b, s, d = x.shape
    cl = ctx.shape[1]
    depth = w_ada.shape[0]
    mix_w = d // 2
    n_lat, n_ctx = b * s, b * cl
    n_all = n_lat + n_ctx
    tm = math.gcd(math.gcd(512, s), n_ctx)
    dims = {
        "tm": tm,
        "tm_big": math.gcd(1024, s),
        "s": s,
        "b": b,
        "mod_row": lambda i: jnp.minimum(i * tm // s, b),
        "norm_w": norm_w.reshape(depth * 3, 1, d),
    }

    w13_b, w2_b = ffn_w13.astype(_BF16), ffn_w2.astype(_BF16)
    w_in_b, w_out_b = w_in.astype(_BF16), w_out.astype(_BF16)
    w_branch_b = (w_branch.astype(_BF16).reshape(depth, w_branch.shape[1], mix_w, d // MERGE_TN, MERGE_TN)
                  .transpose(0, 1, 3, 2, 4))

    cvec = jnp.concatenate([c, c_ctx[None, :], jnp.zeros((8 - b - 1, d), _F32)], axis=0)
    mod = _ada_mod(cvec, w_ada, b_ada).reshape(depth, 8, N_MOD, d)

    cum = jnp.cumsum(jax.nn.softmax(hgrn_lb.astype(_F32), axis=1), axis=1)
    lb = cum - cum[:, :1]
    lb_tab = jnp.stack([jnp.log(lb), jnp.log1p(-lb), 1.0 - lb], axis=2)
    w_tab, lmap = _hgrn_tables()
    cos, sin = _rope_tables(s)

    h = jnp.concatenate([x.reshape(n_lat, d), ctx.reshape(n_ctx, d)], axis=0)
    for layer in range(depth):
        last = layer == depth - 1
        lam_init = 0.8 - 0.6 * math.exp(-0.3 * layer)
        mod_l = mod[layer]
        h = _ffn(h, n_all, mod_l, layer * 3, w13_b, w2_b, layer, 0, 0, dims)
        p = _mixer_in(h, mod_l, layer * 3 + 1, w_in_b, layer, dims)
        qk = _rope(p, cos, sin, b, s, 8 * mix_w)
        o_fb = _hgrn(p, lb_tab[:, layer], w_tab, lmap, b, s, cl, mix_w)
        tq = min(512, s)
        c_att = _attn(qk, 0, 0, tq, s // tq, p, qk, b, s, cl, mix_w, da_lambda, da_norm_w, layer, lam_init, True)
        n_rows = n_lat if last else n_all
        c_ctx_att = None
        if not last:
            c_ctx_att = _attn(p, 8 * mix_w, n_lat, cl, 1, p, None, b, s, cl, mix_w, da_lambda, da_norm_w,
                              layer, lam_init, False)
        a_br, b_br = _branches(o_fb, p, n_rows, hgrn_norm_w, conv_w, layer, b, s, cl, mix_w)
        h = _merge(h, n_rows, mod_l, a_br, b_br, c_att, c_ctx_att, p, w_branch_b, w_out_b, layer, dims, mix_w)
        h = _ffn(h, n_rows, mod_l, layer * 3 + 2, w13_b, w2_b, layer, 1, 6, dims,
                 final_w=final_norm_w if last else None)
    return h.reshape(b, s, d)
```
